```python
import math
import jax, jax.numpy as jnp
from jax import lax
import numpy as np

D_MODEL = 1024
BATCH = 16
SEQ = 4096
DEPTH = 2
DEC_BATCH = 2
DEC_SEQ = 8192
PAST_LEN = 128

D_FF = 2816
MIX_WIDTH = D_MODEL
RG_WIDTH = D_MODEL // 4
RG_HEADS = 4
RG_BW = RG_WIDTH // RG_HEADS
RG_C = 8.0
CONV_W = 4
CONV_LEFT = 2
ATT_HEADS = 4
HEAD_DIM = D_MODEL // 16
ATT_WIDTH = ATT_HEADS * 2 * HEAD_DIM
F_WIDTH = MIX_WIDTH - RG_WIDTH - ATT_WIDTH
F_GROUPS = 4
F_GW = F_WIDTH // F_GROUPS
Q_BLOCK = 128
SPLITS = [RG_WIDTH, 2 * RG_WIDTH, 2 * RG_WIDTH + ATT_WIDTH, 2 * RG_WIDTH + 2 * ATT_WIDTH, 2 * RG_WIDTH + 3 * ATT_WIDTH]
V_OFF = 2 * RG_WIDTH + 2 * ATT_WIDTH
IN_WIDTH = 2 * RG_WIDTH + 3 * ATT_WIDTH + F_WIDTH
ALPHA = (2.0 * DEPTH) ** 0.25
BETA = (8.0 * DEPTH) ** -0.25
LN_EPS = 1e-5
NORM_EPS = 1e-5

kernel_name = 'hymba_rglru_diffattn_fnet_macaron_encoder'


def layer_norm(x, g, b):
    xf = x.astype(jnp.float32)
    mu = jnp.mean(xf, -1, keepdims=True)
    xc = xf - mu
    var = jnp.mean(xc * xc, -1, keepdims=True)
    return (xc * lax.rsqrt(var + LN_EPS) * g.astype(jnp.float32) + b.astype(jnp.float32)).astype(x.dtype)


def swiglu(x, wg, wu, wd):
    hg = jnp.einsum('bsd,df->bsf', x, wg)
    hu = jnp.einsum('bsd,df->bsf', x, wu)
    return jnp.einsum('bsf,fd->bsd', jax.nn.silu(hg) * hu, wd)


def centred_dwconv(x, w, b):
    S = x.shape[1]
    xp = jnp.pad(x, ((0, 0), (CONV_LEFT, CONV_W - 1 - CONV_LEFT), (0, 0)))
    y = b
    for j in range(CONV_W):
        y = y + xp[:, j:j + S, :] * w[j]
    return y


def linear_scan(a, u):
    def combine(left, right):
        a_l, u_l = left
        a_r, u_r = right
        return a_l * a_r, a_r * u_l + u_r
    _, h = lax.associative_scan(combine, (a, u), axis=1)
    return h


def rglru_direction(x, wa, ba, wx, bx, lam):
    B, S, _ = x.shape
    xh = x.reshape(B, S, RG_HEADS, RG_BW)
    r = jax.nn.sigmoid(jnp.einsum('bshi,hij->bshj', xh, wa.astype(jnp.float32)).reshape(B, S, RG_WIDTH) + ba.astype(jnp.float32))
    i = jax.nn.sigmoid(jnp.einsum('bshi,hij->bshj', xh, wx.astype(jnp.float32)).reshape(B, S, RG_WIDTH) + bx.astype(jnp.float32))
    log_a = -RG_C * jax.nn.softplus(-lam.astype(jnp.float32)) * r
    a = jnp.exp(log_a)
    mult = jnp.sqrt(-jnp.expm1(2.0 * log_a))
    return linear_scan(a, mult * (i * x))


def bidir_rglru(x, wa, ba, wx, bx, lam):
    fwd = rglru_direction(x, wa[0], ba[0], wx[0], bx[0], lam[0])
    bwd = jnp.flip(rglru_direction(jnp.flip(x, 1), wa[1], ba[1], wx[1], bx[1], lam[1]), 1)
    return fwd + bwd


def diff_attention(q, k, v, lam, subln_g, lam_init):
    B, S = q.shape[0], q.shape[1]
    nb = S // Q_BLOCK
    slopes = 2.0 ** (-8.0 * jnp.arange(1, ATT_HEADS + 1, dtype=jnp.float32) / ATT_HEADS)
    scale = HEAD_DIM ** -0.5
    pos_k = jnp.arange(S, dtype=jnp.int32)
    qb = jnp.moveaxis(q.reshape(B, nb, Q_BLOCK, ATT_HEADS, 2, HEAD_DIM), 1, 0)
    starts = jnp.arange(nb, dtype=jnp.int32) * Q_BLOCK

    def block(args):
        qi, start = args
        pos_q = start + jnp.arange(Q_BLOCK, dtype=jnp.int32)
        dist = jnp.abs(pos_q[:, None] - pos_k[None, :]).astype(jnp.float32)
        bias = -slopes[:, None, None] * dist
        s = jnp.einsum('bqhcd,bkhcd->bhcqk', qi, k, preferred_element_type=jnp.float32) * scale + bias[None, :, None]
        p = jax.nn.softmax(s, axis=-1)
        w = p[:, :, 0] - lam * p[:, :, 1]
        return jnp.einsum('bhqk,bkhe->bqhe', w.astype(v.dtype), v, preferred_element_type=jnp.float32)

    o = lax.map(block, (qb, starts))
    o = jnp.moveaxis(o, 0, 1).reshape(B, S, ATT_HEADS, 2 * HEAD_DIM)
    o = o * lax.rsqrt(jnp.mean(o * o, -1, keepdims=True) + NORM_EPS) * subln_g.astype(jnp.float32) * (1.0 - lam_init)
    return o.reshape(B, S, ATT_WIDTH)


def fourier_mix(f):
    B, S, _ = f.shape
    fg = f.astype(jnp.float32).reshape(B, S, F_GROUPS, F_GW)
    out = jnp.fft.fft2(fg, axes=(1, 3), norm='ortho').real
    return out.reshape(B, S, F_WIDTH)


def hybrid_mixer(h, w_in, conv_w, conv_b, rg_wa, rg_ba, rg_wx, rg_bx, rg_lambda, lambda_qk, subln_g, w_out, lam_init):
    B, S, _ = h.shape
    proj = jnp.einsum('bsd,de->bse', h, w_in)
    rx, rgate, q, k, v, fx = jnp.split(proj, SPLITS, axis=-1)
    xc = centred_dwconv(rx, conv_w, conv_b).astype(jnp.float32)
    out_a = jax.nn.gelu(rgate.astype(jnp.float32)) * bidir_rglru(xc, rg_wa, rg_ba, rg_wx, rg_bx, rg_lambda)
    lq = lambda_qk.astype(jnp.float32)
    lam = jnp.exp(jnp.sum(lq[0] * lq[1])) - jnp.exp(jnp.sum(lq[2] * lq[3])) + lam_init
    out_b = diff_attention(q.reshape(B, S, ATT_HEADS, 2, HEAD_DIM), k.reshape(B, S, ATT_HEADS, 2, HEAD_DIM),
                           v.reshape(B, S, ATT_HEADS, 2 * HEAD_DIM), lam, subln_g, lam_init)
    out_c = fourier_mix(fx)
    y = jnp.concatenate([out_a, out_b, out_c], axis=-1).astype(h.dtype)
    return jnp.einsum('bse,ed->bsd', y, w_out)


def encoder_trunk(x, ln_g, ln_b, ffn1_wg, ffn1_wu, ffn1_wd, ffn2_wg, ffn2_wu, ffn2_wd, w_in, conv_w, conv_b,
                  rg_wa, rg_ba, rg_wx, rg_bx, rg_lambda, lambda_qk, subln_g, w_out):
    for l in range(DEPTH):
        lam_init = 0.8 - 0.6 * math.exp(-0.3 * l)
        x = layer_norm(ALPHA * x + 0.5 * swiglu(x, ffn1_wg[l], ffn1_wu[l], ffn1_wd[l]), ln_g[l, 0], ln_b[l, 0])
        x = layer_norm(ALPHA * x + hybrid_mixer(x, w_in[l], conv_w[l], conv_b[l], rg_wa[l], rg_ba[l], rg_wx[l], rg_bx[l],
                                                rg_lambda[l], lambda_qk[l], subln_g[l], w_out[l], lam_init),
                       ln_g[l, 1], ln_b[l, 1])
        x = layer_norm(ALPHA * x + 0.5 * swiglu(x, ffn2_wg[l], ffn2_wu[l], ffn2_wd[l]), ln_g[l, 2], ln_b[l, 2])
    return x


def setup_inputs(seed: int = 0) -> dict:
    key = jax.random.key(seed)
    ks = jax.random.split(key, 24)
    f32 = jnp.float32

    def nrm(k, shape, scale):
        return jax.random.normal(k, shape, f32) * scale

    x_prompt = nrm(ks[0], (BATCH, SEQ, D_MODEL), 1.0)
    x_sample = nrm(ks[1], (DEC_BATCH, DEC_SEQ, D_MODEL), 1.0)
    ln_g = 1.0 + nrm(ks[2], (DEPTH, 3, D_MODEL), 0.02)
    ln_b = nrm(ks[3], (DEPTH, 3, D_MODEL), 0.02)
    ffn1_wg = nrm(ks[4], (DEPTH, D_MODEL, D_FF), D_MODEL ** -0.5)
    ffn1_wu = nrm(ks[5], (DEPTH, D_MODEL, D_FF), D_MODEL ** -0.5)
    ffn1_wd = nrm(ks[6], (DEPTH, D_FF, D_MODEL), D_FF ** -0.5 * BETA)
    ffn2_wg = nrm(ks[7], (DEPTH, D_MODEL, D_FF), D_MODEL ** -0.5)
    ffn2_wu = nrm(ks[8], (DEPTH, D_MODEL, D_FF), D_MODEL ** -0.5)
    ffn2_wd = nrm(ks[9], (DEPTH, D_FF, D_MODEL), D_FF ** -0.5 * BETA)
    col_scale = jnp.ones((IN_WIDTH,), f32).at[V_OFF:V_OFF + ATT_WIDTH].set(BETA)
    w_in = nrm(ks[10], (DEPTH, D_MODEL, IN_WIDTH), D_MODEL ** -0.5) * col_scale
    conv_w = nrm(ks[11], (DEPTH, CONV_W, RG_WIDTH), CONV_W ** -0.5)
    conv_b = nrm(ks[12], (DEPTH, RG_WIDTH), 0.01)
    rg_wa = nrm(ks[13], (DEPTH, 2, RG_HEADS, RG_BW, RG_BW), RG_BW ** -0.5)
    rg_ba = nrm(ks[14], (DEPTH, 2, RG_WIDTH), 0.01)
    rg_wx = nrm(ks[15], (DEPTH, 2, RG_HEADS, RG_BW, RG_BW), RG_BW ** -0.5)
    rg_bx = nrm(ks[16], (DEPTH, 2, RG_WIDTH), 0.01)
    a_target = jax.random.uniform(ks[17], (DEPTH, 2, RG_WIDTH), f32, minval=0.9, maxval=0.999)
    base = a_target ** (1.0 / RG_C)
    rg_lambda = jnp.log(base) - jnp.log1p(-base)
    lambda_qk = nrm(ks[18], (DEPTH, 4, HEAD_DIM), 0.1)
    subln_g = 1.0 + nrm(ks[19], (DEPTH, 2 * HEAD_DIM), 0.02)
    w_out = nrm(ks[20], (DEPTH, MIX_WIDTH, D_MODEL), MIX_WIDTH ** -0.5 * BETA)
    return {'x_prompt': x_prompt, 'x_sample': x_sample, 'ln_g': ln_g, 'ln_b': ln_b,
            'ffn1_wg': ffn1_wg, 'ffn1_wu': ffn1_wu, 'ffn1_wd': ffn1_wd,
            'ffn2_wg': ffn2_wg, 'ffn2_wu': ffn2_wu, 'ffn2_wd': ffn2_wd,
            'w_in': w_in, 'conv_w': conv_w, 'conv_b': conv_b,
            'rg_wa': rg_wa, 'rg_ba': rg_ba, 'rg_wx': rg_wx, 'rg_bx': rg_bx, 'rg_lambda': rg_lambda,
            'lambda_qk': lambda_qk, 'subln_g': subln_g, 'w_out': w_out}


def reference(x_prompt, x_sample, ln_g, ln_b, ffn1_wg, ffn1_wu, ffn1_wd, ffn2_wg, ffn2_wu, ffn2_wd, w_in, conv_w, conv_b,
              rg_wa, rg_ba, rg_wx, rg_bx, rg_lambda, lambda_qk, subln_g, w_out):
    y_prompt = encoder_trunk(x_prompt, ln_g, ln_b, ffn1_wg, ffn1_wu, ffn1_wd, ffn2_wg, ffn2_wu, ffn2_wd, w_in, conv_w, conv_b,
                             rg_wa, rg_ba, rg_wx, rg_bx, rg_lambda, lambda_qk, subln_g, w_out)
    y_sample = encoder_trunk(x_sample, ln_g, ln_b, ffn1_wg, ffn1_wu, ffn1_wd, ffn2_wg, ffn2_wu, ffn2_wd, w_in, conv_w, conv_b,
                             rg_wa, rg_ba, rg_wx, rg_bx, rg_lambda, lambda_qk, subln_g, w_out)
    return (y_prompt, y_sample)
```

```python
import functools
import math

import jax
import jax.numpy as jnp
from jax import lax
from jax.experimental import pallas as pl
from jax.experimental.pallas import tpu as pltpu

D_MODEL = 1024
DEPTH = 2
D_FF = 2816
RG_WIDTH = 256
RG_HEADS = 4
RG_BW = 64
RG_C = 8.0
CONV_W = 4
ATT_HEADS = 4
HEAD_DIM = 64
ATT_WIDTH = 512
F_WIDTH = 256
F_GROUPS = 4
F_GW = 64
IN_WIDTH = 2304
ALPHA = (2.0 * DEPTH) ** 0.25
LN_EPS = 1e-5
NORM_EPS = 1e-5

BF16 = jnp.bfloat16
F32 = jnp.float32

VMEM_LIMIT_BYTES = 56 * 1024 * 1024
SUBLANES = 8

TOKEN_TILE = 512
FF_CHUNKS = ((0, 1024), (1024, 2048), (2048, 2816))
SCAN_CHUNK = 1024
ATT_TQ = 256
ATT_TK = 256
DFT_N1 = 1024
DFT_TM = 256


def _cparams(n_axes):
    return pltpu.CompilerParams(
        dimension_semantics=("arbitrary",) * n_axes,
        vmem_limit_bytes=VMEM_LIMIT_BYTES,
    )


def _dot(a, b):
    return jnp.dot(a, b, preferred_element_type=F32)


def _layer_norm(y, g, b):
    mu = jnp.mean(y, axis=-1, keepdims=True)
    yc = y - mu
    var = jnp.mean(yc * yc, axis=-1, keepdims=True)
    return yc * lax.rsqrt(var + LN_EPS) * g + b


def _swiglu(xb, wg_ref, wu_ref, wd_ref):
    acc = None
    for lo, hi in FF_CHUNKS:
        hg = _dot(xb, wg_ref[:, lo:hi])
        hu = _dot(xb, wu_ref[:, lo:hi])
        h = (hg * jax.nn.sigmoid(hg) * hu).astype(BF16)
        part = _dot(h, wd_ref[lo:hi, :])
        acc = part if acc is None else acc + part
    return acc


def _ffn_inproj_kernel(x_ref, wg_ref, wu_ref, wd_ref, g_ref, b_ref, win_ref,
                       xo_ref, rxg_ref, qkv_ref, fx_ref):
    x = x_ref[...]
    y = ALPHA * x + 0.5 * _swiglu(x.astype(BF16), wg_ref, wu_ref, wd_ref)
    x1 = _layer_norm(y, g_ref[...], b_ref[...])
    xo_ref[...] = x1
    xb = x1.astype(BF16)
    rxg_ref[...] = _dot(xb, win_ref[:, 0:2 * RG_WIDTH])
    qkv_ref[...] = _dot(xb, win_ref[:, 2 * RG_WIDTH:2 * RG_WIDTH + 3 * ATT_WIDTH]).astype(BF16)
    fx_ref[...] = _dot(xb, win_ref[:, 2 * RG_WIDTH + 3 * ATT_WIDTH:IN_WIDTH]).astype(BF16)


def _full(shape):
    return pl.BlockSpec(shape, lambda *_: (0,) * len(shape))


def _ffn_inproj(x, wg, wu, wd, g, b, win):
    t = x.shape[0]
    tm = TOKEN_TILE
    row = lambda w: pl.BlockSpec((tm, w), lambda i: (i, 0))
    return pl.pallas_call(
        _ffn_inproj_kernel,
        grid=(t // tm,),
        in_specs=[row(D_MODEL), _full(wg.shape), _full(wu.shape), _full(wd.shape),
                  _full(g.shape), _full(b.shape), _full(win.shape)],
        out_specs=[row(D_MODEL), row(2 * RG_WIDTH), row(3 * ATT_WIDTH), row(F_WIDTH)],
        out_shape=[jax.ShapeDtypeStruct((t, D_MODEL), F32),
                   jax.ShapeDtypeStruct((t, 2 * RG_WIDTH), F32),
                   jax.ShapeDtypeStruct((t, 3 * ATT_WIDTH), BF16),
                   jax.ShapeDtypeStruct((t, F_WIDTH), BF16)],
        compiler_params=_cparams(1),
        name="ffn_inproj",
    )(x, wg, wu, wd, g, b, win)


def _outproj_ffn_kernel(x_ref, hf_ref, hb_ref, rg_ref, yb_ref, yc_ref, wo_ref, g1_ref, b1_ref,
                        wg_ref, wu_ref, wd_ref, g2_ref, b2_ref, o_ref):
    x = x_ref[...]
    ya = (jax.nn.gelu(rg_ref[...], approximate=True) * (hf_ref[...] + hb_ref[...])).astype(BF16)
    mix = (_dot(ya, wo_ref[0:RG_WIDTH, :])
           + _dot(yb_ref[...], wo_ref[RG_WIDTH:RG_WIDTH + ATT_WIDTH, :])
           + _dot(yc_ref[...], wo_ref[RG_WIDTH + ATT_WIDTH:D_MODEL, :]))
    x2 = _layer_norm(ALPHA * x + mix, g1_ref[...], b1_ref[...])
    y = ALPHA * x2 + 0.5 * _swiglu(x2.astype(BF16), wg_ref, wu_ref, wd_ref)
    o_ref[...] = _layer_norm(y, g2_ref[...], b2_ref[...])


def _outproj_ffn(x, hf, hb, rxg, yb, yc, wo, g1, b1, wg, wu, wd, g2, b2):
    t = x.shape[0]
    tm = TOKEN_TILE
    row = lambda w: pl.BlockSpec((tm, w), lambda i: (i, 0))
    rgate = pl.BlockSpec((tm, RG_WIDTH), lambda i: (i, 1))
    return pl.pallas_call(
        _outproj_ffn_kernel,
        grid=(t // tm,),
        in_specs=[row(D_MODEL), row(RG_WIDTH), row(RG_WIDTH), rgate, row(ATT_WIDTH), row(F_WIDTH),
                  _full(wo.shape), _full(g1.shape), _full(b1.shape),
                  _full(wg.shape), _full(wu.shape), _full(wd.shape), _full(g2.shape), _full(b2.shape)],
        out_specs=row(D_MODEL),
        out_shape=jax.ShapeDtypeStruct((t, D_MODEL), F32),
        compiler_params=_cparams(1),
        name="outproj_ffn",
    )(x, hf, hb, rxg, yb, yc, wo, g1, b1, wg, wu, wd, g2, b2)


def _rglru_direction(x_ref, prev_ref, next_ref, chunk, n_chunks, d, reverse,
                     cw_ref, cb_ref, wgate_ref, bgate_ref, lam_ref,
                     a_scr, u_scr, carry_scr, h_ref):
    tc = x_ref.shape[0]
    x = x_ref[...]
    prev = prev_ref[...] * jnp.where(chunk > 0, 1.0, 0.0)
    nxt = next_ref[...] * jnp.where(chunk < n_chunks - 1, 1.0, 0.0)
    row = lax.broadcasted_iota(jnp.int32, (tc, 1), 0)
    xm1 = jnp.where(row == 0, prev[7:8, :], pltpu.roll(x, 1, 0))
    xm2 = jnp.where(row == 0, prev[6:7, :], jnp.where(row == 1, prev[7:8, :], pltpu.roll(x, 2, 0)))
    xp1 = jnp.where(row == tc - 1, nxt[0:1, :], pltpu.roll(x, tc - 1, 0))
    cw = cw_ref[...]
    xc = cb_ref[...] + xm2 * cw[0:1, :] + xm1 * cw[1:2, :] + x * cw[2:3, :] + xp1 * cw[3:4, :]

    gates = _dot(xc.astype(BF16), wgate_ref[d]) + bgate_ref[d]
    r = jax.nn.sigmoid(gates[:, 0:RG_WIDTH])
    i = jax.nn.sigmoid(gates[:, RG_WIDTH:2 * RG_WIDTH])
    nlam = -lam_ref[d]
    softplus = jnp.maximum(nlam, 0.0) + jnp.log1p(jnp.exp(-jnp.abs(nlam)))
    log_a = (-RG_C * softplus) * r
    a = jnp.exp(log_a)
    th = jnp.tanh(log_a)
    u = jnp.sqrt(-2.0 * th / (1.0 - th)) * (i * xc)

    rig = row & (SUBLANES - 1)
    for k in (1, 2, 4):
        if reverse:
            keep = rig < SUBLANES - k
            shift = tc - k
        else:
            keep = rig >= k
            shift = k
        us = jnp.where(keep, pltpu.roll(u, shift, 0), 0.0)
        as_ = jnp.where(keep, pltpu.roll(a, shift, 0), 1.0)
        u = u + a * us
        a = a * as_
    a_scr[...] = a
    u_scr[...] = u

    n_groups = tc // SUBLANES

    def body(g, h):
        gi = n_groups - 1 - g if reverse else g
        r0 = pl.multiple_of(gi * SUBLANES, SUBLANES)
        hg = u_scr[pl.ds(r0, SUBLANES), :] + a_scr[pl.ds(r0, SUBLANES), :] * h
        h_ref[pl.ds(r0, SUBLANES), :] = hg
        edge = hg[0:1, :] if reverse else hg[SUBLANES - 1:SUBLANES, :]
        return jnp.broadcast_to(edge, (SUBLANES, RG_WIDTH))

    carry_scr[...] = lax.fori_loop(0, n_groups, body, carry_scr[...], unroll=8)


def _rglru_kernel(xf_ref, pf_ref, nf_ref, xb_ref, pb_ref, nb_ref,
                  cw_ref, cb_ref, wgate_ref, bgate_ref, lam_ref,
                  hf_ref, hb_ref, af_scr, uf_scr, ab_scr, ub_scr, cf_scr, cbk_scr, *, n_chunks):
    c = pl.program_id(1)

    @pl.when(c == 0)
    def _():
        cf_scr[...] = jnp.zeros_like(cf_scr)
        cbk_scr[...] = jnp.zeros_like(cbk_scr)

    _rglru_direction(xf_ref, pf_ref, nf_ref, c, n_chunks, 0, False,
                     cw_ref, cb_ref, wgate_ref, bgate_ref, lam_ref, af_scr, uf_scr, cf_scr, hf_ref)
    _rglru_direction(xb_ref, pb_ref, nb_ref, n_chunks - 1 - c, n_chunks, 1, True,
                     cw_ref, cb_ref, wgate_ref, bgate_ref, lam_ref, ab_scr, ub_scr, cbk_scr, hb_ref)


def _rglru(rxg, batch, seq, cw, cb, wgate, bgate, lam):
    t = rxg.shape[0]
    tc = SCAN_CHUNK
    nch = seq // tc
    hb8 = tc // SUBLANES
    last8 = t // SUBLANES - 1

    def main_f(b, c):
        return (b * nch + c, 0)

    def prev_f(b, c):
        return (jnp.maximum((b * nch + c) * hb8 - 1, 0), 0)

    def next_f(b, c):
        return (jnp.minimum((b * nch + c + 1) * hb8, last8), 0)

    def main_b(b, c):
        return (b * nch + (nch - 1 - c), 0)

    def prev_b(b, c):
        return (jnp.maximum((b * nch + (nch - 1 - c)) * hb8 - 1, 0), 0)

    def next_b(b, c):
        return (jnp.minimum((b * nch + (nch - 1 - c) + 1) * hb8, last8), 0)

    blk = lambda f: pl.BlockSpec((tc, RG_WIDTH), f)
    halo = lambda f: pl.BlockSpec((SUBLANES, RG_WIDTH), f)
    return pl.pallas_call(
        functools.partial(_rglru_kernel, n_chunks=nch),
        grid=(batch, nch),
        in_specs=[blk(main_f), halo(prev_f), halo(next_f), blk(main_b), halo(prev_b), halo(next_b),
                  _full(cw.shape), _full(cb.shape), _full(wgate.shape), _full(bgate.shape), _full(lam.shape)],
        out_specs=[blk(main_f), blk(main_b)],
        out_shape=[jax.ShapeDtypeStruct((t, RG_WIDTH), F32), jax.ShapeDtypeStruct((t, RG_WIDTH), F32)],
        scratch_shapes=[pltpu.VMEM((tc, RG_WIDTH), F32)] * 4 + [pltpu.VMEM((SUBLANES, RG_WIDTH), F32)] * 2,
        compiler_params=_cparams(2),
        name="rglru",
    )(rxg, rxg, rxg, rxg, rxg, rxg, cw, cb, wgate, bgate, lam)


def _attn_kernel(slopes_ref, q_ref, k_ref, v_ref, lq_ref, g_ref, o_ref, vt_scr, *, seq, tq, tk, lam_init):
    h = pl.program_id(1)
    qi = pl.program_id(2)
    nk = seq // tk

    @pl.when(qi == 0)
    def _():
        for j in range(nk):
            vt_scr[j] = v_ref[j * tk:(j + 1) * tk, :].astype(F32).T.astype(BF16)

    slope = slopes_ref[h]
    q = q_ref[...] * (HEAD_DIM ** -0.5)
    lane = lax.broadcasted_iota(jnp.int32, q.shape, 1)
    qmaps = (jnp.where(lane < HEAD_DIM, q, 0.0).astype(BF16),
             jnp.where(lane >= HEAD_DIM, q, 0.0).astype(BF16))
    pos_q = qi * tq + lax.broadcasted_iota(jnp.int32, (1, tq), 1)

    def body(j, carry):
        kt = k_ref[pl.ds(pl.multiple_of(j * tk, tk), tk), :]
        vt = vt_scr[j]
        pos_k = j * tk + lax.broadcasted_iota(jnp.int32, (tk, 1), 0)
        bias = -slope * jnp.abs(pos_k - pos_q).astype(F32)
        out = []
        for c in range(2):
            m, l, o = carry[3 * c:3 * c + 3]
            s = lax.dot_general(kt, qmaps[c], (((1,), (1,)), ((), ())), preferred_element_type=F32) + bias
            mn = jnp.maximum(m, jnp.max(s, axis=0, keepdims=True))
            alpha = jnp.exp(m - mn)
            p = jnp.exp(s - mn)
            l = alpha * l + jnp.sum(p, axis=0, keepdims=True)
            o = alpha * o + _dot(vt, p.astype(BF16))
            out += [mn, l, o]
        return tuple(out)

    init = (jnp.full((1, tq), -1e30, F32), jnp.zeros((1, tq), F32), jnp.zeros((2 * HEAD_DIM, tq), F32)) * 2
    m0, l0, o0, m1, l1, o1 = lax.fori_loop(0, nk, body, init)

    lq = lq_ref[...]
    lam = (jnp.exp(jnp.sum(lq[0:1, :] * lq[1:2, :], axis=1, keepdims=True))
           - jnp.exp(jnp.sum(lq[2:3, :] * lq[3:4, :], axis=1, keepdims=True)) + lam_init)
    o = o0 / l0 - lam * (o1 / l1)
    o = o * lax.rsqrt(jnp.mean(o * o, axis=0, keepdims=True) + NORM_EPS) * g_ref[...] * (1.0 - lam_init)
    o_ref[...] = o.T.astype(BF16)


def _attention(qkv, batch, seq, slopes, lq, g_col, lam_init):
    t = qkv.shape[0]
    tq, tk = ATT_TQ, ATT_TK
    nq = seq // tq
    kernel = functools.partial(_attn_kernel, seq=seq, tq=tq, tk=tk, lam_init=lam_init)
    return pl.pallas_call(
        kernel,
        grid=(batch, ATT_HEADS, nq),
        in_specs=[pl.BlockSpec(memory_space=pltpu.SMEM),
                  pl.BlockSpec((tq, 2 * HEAD_DIM), lambda b, h, i: (b * nq + i, h)),
                  pl.BlockSpec((seq, 2 * HEAD_DIM), lambda b, h, i: (b, ATT_HEADS + h)),
                  pl.BlockSpec((seq, 2 * HEAD_DIM), lambda b, h, i: (b, 2 * ATT_HEADS + h)),
                  _full(lq.shape), _full(g_col.shape)],
        out_specs=pl.BlockSpec((tq, 2 * HEAD_DIM), lambda b, h, i: (b * nq + i, h)),
        out_shape=jax.ShapeDtypeStruct((t, ATT_WIDTH), BF16),
        scratch_shapes=[pltpu.VMEM((seq // tk, 2 * HEAD_DIM, tk), BF16)],
        compiler_params=_cparams(3),
        name="diff_attention",
    )(slopes, qkv, qkv, qkv, lq, g_col)


def _snap(v):
    r = round(v)
    return float(r) if abs(v - r) < 1e-12 else v


def _fourier_kernel(x_ref, fc_ref, fs_ref, twc_ref, tws_ref, cc_ref, sc_ref, o_ref, *, n2, scale):
    x = x_ref[...]
    ar = _dot(fc_ref[...], x)
    ai = _dot(fs_ref[...], x)
    c = twc_ref[...]
    s = tws_ref[...]
    br = ar * c - ai * s
    bi = -(ar * s + ai * c)
    cc = cc_ref[...]
    sc = sc_ref[...]
    for k2 in range(n2):
        xr = None
        xi = None
        for m in range(n2):
            ph = (m * k2) % n2
            cp = _snap(math.cos(2.0 * math.pi * ph / n2))
            sp = _snap(math.sin(2.0 * math.pi * ph / n2))
            brn = br[:, m * F_WIDTH:(m + 1) * F_WIDTH]
            bin_ = bi[:, m * F_WIDTH:(m + 1) * F_WIDTH]
            tr = brn * cp + bin_ * sp
            ti = bin_ * cp - brn * sp
            xr = tr if xr is None else xr + tr
            xi = ti if xi is None else xi + ti
        out = (_dot(xr.astype(BF16), cc) + _dot(xi.astype(BF16), sc)) * scale
        o_ref[k2] = out.astype(BF16)


def _fourier(fx, batch, seq, fc, fs, cc, sc):
    n1 = DFT_N1
    n2 = seq // n1
    tm = DFT_TM
    x2 = fx.reshape(batch, n1, n2 * F_WIDTH)
    k1 = lax.broadcasted_iota(jnp.int32, (n1, n2 * F_WIDTH), 0)
    m = lax.broadcasted_iota(jnp.int32, (n1, n2 * F_WIDTH), 1) // F_WIDTH
    ang = ((k1 * m) % seq).astype(F32) * (2.0 * math.pi / seq)
    twc = jnp.cos(ang)
    tws = jnp.sin(ang)
    kernel = functools.partial(_fourier_kernel, n2=n2, scale=1.0 / math.sqrt(seq * F_GW))
    out = pl.pallas_call(
        kernel,
        grid=(n1 // tm, batch),
        in_specs=[pl.BlockSpec((None, n1, n2 * F_WIDTH), lambda i, b: (b, 0, 0)),
                  pl.BlockSpec((tm, n1), lambda i, b: (i, 0)),
                  pl.BlockSpec((tm, n1), lambda i, b: (i, 0)),
                  pl.BlockSpec((tm, n2 * F_WIDTH), lambda i, b: (i, 0)),
                  pl.BlockSpec((tm, n2 * F_WIDTH), lambda i, b: (i, 0)),
                  _full(cc.shape), _full(sc.shape)],
        out_specs=pl.BlockSpec((None, n2, tm, F_WIDTH), lambda i, b: (b, 0, i, 0)),
        out_shape=jax.ShapeDtypeStruct((batch, n2, n1, F_WIDTH), BF16),
        compiler_params=_cparams(2),
        name="fourier",
    )(x2, fc, fs, twc, tws, cc, sc)
    return out.reshape(batch * seq, F_WIDTH)


def _block_diag(w):
    eye = jnp.eye(RG_HEADS, dtype=w.dtype)
    return (eye[:, None, :, None] * w[:, :, None, :]).reshape(RG_WIDTH, RG_WIDTH)


def _dft_tables():
    n = lax.broadcasted_iota(jnp.int32, (DFT_N1, DFT_N1), 0)
    k = lax.broadcasted_iota(jnp.int32, (DFT_N1, DFT_N1), 1)
    ang = ((n * k) % DFT_N1).astype(F32) * (2.0 * math.pi / DFT_N1)
    fc = jnp.cos(ang).astype(BF16)
    fs = jnp.sin(ang).astype(BF16)
    j = lax.broadcasted_iota(jnp.int32, (F_WIDTH, F_WIDTH), 0)
    jp = lax.broadcasted_iota(jnp.int32, (F_WIDTH, F_WIDTH), 1)
    same = (j // F_GW) == (jp // F_GW)
    ang_c = (((j % F_GW) * (jp % F_GW)) % F_GW).astype(F32) * (2.0 * math.pi / F_GW)
    cc = jnp.where(same, jnp.cos(ang_c), 0.0).astype(BF16)
    sc = jnp.where(same, jnp.sin(ang_c), 0.0).astype(BF16)
    return fc, fs, cc, sc


def _trunk(x, batch, seq, p, tables):
    fc, fs, cc, sc = tables
    slopes = 2.0 ** (-8.0 * jnp.arange(1, ATT_HEADS + 1, dtype=F32) / ATT_HEADS)
    for l in range(DEPTH):
        lam_init = 0.8 - 0.6 * math.exp(-0.3 * l)
        x1, rxg, qkv, fx = _ffn_inproj(x, p["ffn1_wg"][l], p["ffn1_wu"][l], p["ffn1_wd"][l],
                                       p["ln_g"][l, 0:1], p["ln_b"][l, 0:1], p["w_in"][l])
        hf, hb = _rglru(rxg, batch, seq, p["conv_w"][l], p["conv_b"][l], p["wgate"][l], p["bgate"][l],
                        p["rg_lambda"][l])
        yb = _attention(qkv, batch, seq, slopes, p["lambda_qk"][l], p["subln_g"][l], lam_init)
        yc = _fourier(fx, batch, seq, fc, fs, cc, sc)
        x = _outproj_ffn(x1, hf, hb, rxg, yb, yc, p["w_out"][l], p["ln_g"][l, 1:2], p["ln_b"][l, 1:2],
                         p["ffn2_wg"][l], p["ffn2_wu"][l], p["ffn2_wd"][l], p["ln_g"][l, 2:3], p["ln_b"][l, 2:3])
    return x


def kernel(x_prompt, x_sample, ln_g, ln_b, ffn1_wg, ffn1_wu, ffn1_wd, ffn2_wg, ffn2_wu, ffn2_wd, w_in, conv_w, conv_b,
           rg_wa, rg_ba, rg_wx, rg_bx, rg_lambda, lambda_qk, subln_g, w_out):
    wgate = jnp.stack([
        jnp.stack([jnp.concatenate([_block_diag(rg_wa[l, d]), _block_diag(rg_wx[l, d])], axis=1)
                   for d in range(2)]) for l in range(DEPTH)]).astype(BF16)
    bgate = jnp.concatenate([rg_ba, rg_bx], axis=-1)[:, :, None, :]
    p = dict(
        ln_g=ln_g, ln_b=ln_b,
        ffn1_wg=ffn1_wg.astype(BF16), ffn1_wu=ffn1_wu.astype(BF16), ffn1_wd=ffn1_wd.astype(BF16),
        ffn2_wg=ffn2_wg.astype(BF16), ffn2_wu=ffn2_wu.astype(BF16), ffn2_wd=ffn2_wd.astype(BF16),
        w_in=w_in.astype(BF16), w_out=w_out.astype(BF16),
        conv_w=conv_w, conv_b=conv_b[:, None, :],
        wgate=wgate, bgate=bgate, rg_lambda=rg_lambda[:, :, None, :],
        lambda_qk=lambda_qk, subln_g=subln_g[:, :, None],
    )
    tables = _dft_tables()
    outs = []
    for x in (x_prompt, x_sample):
        batch, seq, _ = x.shape
        y = _trunk(x.reshape(batch * seq, D_MODEL), batch, seq, p, tables)
        outs.append(y.reshape(batch, seq, D_MODEL))
    return tuple(outs)
```

```python
import functools
import math

import jax
import jax.numpy as jnp
from jax import lax
from jax.experimental import pallas as pl
from jax.experimental.pallas import tpu as pltpu

D_MODEL = 1024
DEPTH = 2
D_FF = 2816
RG_WIDTH = 256
RG_HEADS = 4
RG_BW = 64
RG_C = 8.0
CONV_W = 4
ATT_HEADS = 4
HEAD_DIM = 64
ATT_WIDTH = 512
F_WIDTH = 256
F_GROUPS = 4
F_GW = 64
IN_WIDTH = 2304
ALPHA = (2.0 * DEPTH) ** 0.25
LN_EPS = 1e-5
NORM_EPS = 1e-5

BF16 = jnp.bfloat16
F32 = jnp.float32

VMEM_LIMIT_BYTES = 56 * 1024 * 1024
SUBLANES = 8

TOKEN_TILE = 512
FF_CHUNKS = ((0, 1024), (1024, 2048), (2048, 2816))
SCAN_CHUNK = 1024
ATT_TQ = 256
ATT_TK = 512
ATT_UNROLL = 2
POS_BLOCK = 256
KA_WIDTH = 256
VT_ROWS = 144
DFT_N1 = 1024
DFT_TM = 256


def _cparams(n_axes):
    return pltpu.CompilerParams(
        dimension_semantics=("arbitrary",) * n_axes,
        vmem_limit_bytes=VMEM_LIMIT_BYTES,
    )


def _dot(a, b):
    return jnp.dot(a, b, preferred_element_type=F32)


def _layer_norm(y, g, b):
    mu = jnp.mean(y, axis=-1, keepdims=True)
    yc = y - mu
    var = jnp.mean(yc * yc, axis=-1, keepdims=True)
    return yc * lax.rsqrt(var + LN_EPS) * g + b


def _swiglu(xb, wg_ref, wu_ref, wd_ref):
    acc = None
    for lo, hi in FF_CHUNKS:
        hg = _dot(xb, wg_ref[:, lo:hi])
        hu = _dot(xb, wu_ref[:, lo:hi])
        h = (hg * jax.nn.sigmoid(hg) * hu).astype(BF16)
        part = _dot(h, wd_ref[lo:hi, :])
        acc = part if acc is None else acc + part
    return acc


def _ffn_inproj_kernel(x_ref, wg_ref, wu_ref, wd_ref, g_ref, b_ref, win_ref,
                       xo_ref, rxg_ref, qkv_ref, fx_ref):
    x = x_ref[...]
    y = ALPHA * x + 0.5 * _swiglu(x.astype(BF16), wg_ref, wu_ref, wd_ref)
    x1 = _layer_norm(y, g_ref[...], b_ref[...])
    xo_ref[...] = x1
    xb = x1.astype(BF16)
    rxg_ref[...] = _dot(xb, win_ref[:, 0:2 * RG_WIDTH])
    qkv_ref[...] = _dot(xb, win_ref[:, 2 * RG_WIDTH:2 * RG_WIDTH + 3 * ATT_WIDTH]).astype(BF16)
    fx_ref[...] = _dot(xb, win_ref[:, 2 * RG_WIDTH + 3 * ATT_WIDTH:IN_WIDTH]).astype(BF16)


def _full(shape):
    return pl.BlockSpec(shape, lambda *_: (0,) * len(shape))


def _ffn_inproj(x, wg, wu, wd, g, b, win):
    t = x.shape[0]
    tm = TOKEN_TILE
    row = lambda w: pl.BlockSpec((tm, w), lambda i: (i, 0))
    return pl.pallas_call(
        _ffn_inproj_kernel,
        grid=(t // tm,),
        in_specs=[row(D_MODEL), _full(wg.shape), _full(wu.shape), _full(wd.shape),
                  _full(g.shape), _full(b.shape), _full(win.shape)],
        out_specs=[row(D_MODEL), row(2 * RG_WIDTH), row(3 * ATT_WIDTH), row(F_WIDTH)],
        out_shape=[jax.ShapeDtypeStruct((t, D_MODEL), F32),
                   jax.ShapeDtypeStruct((t, 2 * RG_WIDTH), F32),
                   jax.ShapeDtypeStruct((t, 3 * ATT_WIDTH), BF16),
                   jax.ShapeDtypeStruct((t, F_WIDTH), BF16)],
        compiler_params=_cparams(1),
        name="ffn_inproj",
    )(x, wg, wu, wd, g, b, win)


def _outproj_ffn_kernel(x_ref, hf_ref, hb_ref, rg_ref, yb_ref, yc_ref, wo_ref, g1_ref, b1_ref,
                        wg_ref, wu_ref, wd_ref, g2_ref, b2_ref, o_ref):
    x = x_ref[...]
    ya = (jax.nn.gelu(rg_ref[...], approximate=True) * (hf_ref[...] + hb_ref[...])).astype(BF16)
    mix = (_dot(ya, wo_ref[0:RG_WIDTH, :])
           + _dot(yb_ref[...], wo_ref[RG_WIDTH:RG_WIDTH + ATT_WIDTH, :])
           + _dot(yc_ref[...], wo_ref[RG_WIDTH + ATT_WIDTH:D_MODEL, :]))
    x2 = _layer_norm(ALPHA * x + mix, g1_ref[...], b1_ref[...])
    y = ALPHA * x2 + 0.5 * _swiglu(x2.astype(BF16), wg_ref, wu_ref, wd_ref)
    o_ref[...] = _layer_norm(y, g2_ref[...], b2_ref[...])


def _outproj_ffn(x, hf, hb, rxg, yb, yc, wo, g1, b1, wg, wu, wd, g2, b2):
    t = x.shape[0]
    tm = TOKEN_TILE
    row = lambda w: pl.BlockSpec((tm, w), lambda i: (i, 0))
    rgate = pl.BlockSpec((tm, RG_WIDTH), lambda i: (i, 1))
    return pl.pallas_call(
        _outproj_ffn_kernel,
        grid=(t // tm,),
        in_specs=[row(D_MODEL), row(RG_WIDTH), row(RG_WIDTH), rgate, row(ATT_WIDTH), row(F_WIDTH),
                  _full(wo.shape), _full(g1.shape), _full(b1.shape),
                  _full(wg.shape), _full(wu.shape), _full(wd.shape), _full(g2.shape), _full(b2.shape)],
        out_specs=row(D_MODEL),
        out_shape=jax.ShapeDtypeStruct((t, D_MODEL), F32),
        compiler_params=_cparams(1),
        name="outproj_ffn",
    )(x, hf, hb, rxg, yb, yc, wo, g1, b1, wg, wu, wd, g2, b2)


def _rglru_direction(x_ref, prev_ref, next_ref, chunk, n_chunks, d, reverse,
                     cw_ref, cb_ref, wgate_ref, bgate_ref, lam_ref,
                     a_scr, u_scr, carry_scr, h_ref):
    tc = x_ref.shape[0]
    x = x_ref[...]
    prev = prev_ref[...] * jnp.where(chunk > 0, 1.0, 0.0)
    nxt = next_ref[...] * jnp.where(chunk < n_chunks - 1, 1.0, 0.0)
    row = lax.broadcasted_iota(jnp.int32, (tc, 1), 0)
    xm1 = jnp.where(row == 0, prev[7:8, :], pltpu.roll(x, 1, 0))
    xm2 = jnp.where(row == 0, prev[6:7, :], jnp.where(row == 1, prev[7:8, :], pltpu.roll(x, 2, 0)))
    xp1 = jnp.where(row == tc - 1, nxt[0:1, :], pltpu.roll(x, tc - 1, 0))
    cw = cw_ref[...]
    xc = cb_ref[...] + xm2 * cw[0:1, :] + xm1 * cw[1:2, :] + x * cw[2:3, :] + xp1 * cw[3:4, :]

    gates = _dot(xc.astype(BF16), wgate_ref[d]) + bgate_ref[d]
    r = jax.nn.sigmoid(gates[:, 0:RG_WIDTH])
    i = jax.nn.sigmoid(gates[:, RG_WIDTH:2 * RG_WIDTH])
    nlam = -lam_ref[d]
    softplus = jnp.maximum(nlam, 0.0) + jnp.log1p(jnp.exp(-jnp.abs(nlam)))
    log_a = (-RG_C * softplus) * r
    a = jnp.exp(log_a)
    th = jnp.tanh(log_a)
    u = jnp.sqrt(-2.0 * th / (1.0 - th)) * (i * xc)

    rig = row & (SUBLANES - 1)
    for k in (1, 2, 4):
        if reverse:
            keep = rig < SUBLANES - k
            shift = tc - k
        else:
            keep = rig >= k
            shift = k
        us = jnp.where(keep, pltpu.roll(u, shift, 0), 0.0)
        as_ = jnp.where(keep, pltpu.roll(a, shift, 0), 1.0)
        u = u + a * us
        a = a * as_
    a_scr[...] = a
    u_scr[...] = u

    n_groups = tc // SUBLANES

    def body(g, h):
        gi = n_groups - 1 - g if reverse else g
        r0 = pl.multiple_of(gi * SUBLANES, SUBLANES)
        hg = u_scr[pl.ds(r0, SUBLANES), :] + a_scr[pl.ds(r0, SUBLANES), :] * h
        h_ref[pl.ds(r0, SUBLANES), :] = hg
        edge = hg[0:1, :] if reverse else hg[SUBLANES - 1:SUBLANES, :]
        return jnp.broadcast_to(edge, (SUBLANES, RG_WIDTH))

    carry_scr[...] = lax.fori_loop(0, n_groups, body, carry_scr[...], unroll=8)


def _rglru_kernel(xf_ref, pf_ref, nf_ref, xb_ref, pb_ref, nb_ref,
                  cw_ref, cb_ref, wgate_ref, bgate_ref, lam_ref,
                  hf_ref, hb_ref, af_scr, uf_scr, ab_scr, ub_scr, cf_scr, cbk_scr, *, n_chunks):
    c = pl.program_id(1)

    @pl.when(c == 0)
    def _():
        cf_scr[...] = jnp.zeros_like(cf_scr)
        cbk_scr[...] = jnp.zeros_like(cbk_scr)

    _rglru_direction(xf_ref, pf_ref, nf_ref, c, n_chunks, 0, False,
                     cw_ref, cb_ref, wgate_ref, bgate_ref, lam_ref, af_scr, uf_scr, cf_scr, hf_ref)
    _rglru_direction(xb_ref, pb_ref, nb_ref, n_chunks - 1 - c, n_chunks, 1, True,
                     cw_ref, cb_ref, wgate_ref, bgate_ref, lam_ref, ab_scr, ub_scr, cbk_scr, hb_ref)


def _rglru(rxg, batch, seq, cw, cb, wgate, bgate, lam):
    t = rxg.shape[0]
    tc = SCAN_CHUNK
    nch = seq // tc
    hb8 = tc // SUBLANES
    last8 = t // SUBLANES - 1

    def main_f(b, c):
        return (b * nch + c, 0)

    def prev_f(b, c):
        return (jnp.maximum((b * nch + c) * hb8 - 1, 0), 0)

    def next_f(b, c):
        return (jnp.minimum((b * nch + c + 1) * hb8, last8), 0)

    def main_b(b, c):
        return (b * nch + (nch - 1 - c), 0)

    def prev_b(b, c):
        return (jnp.maximum((b * nch + (nch - 1 - c)) * hb8 - 1, 0), 0)

    def next_b(b, c):
        return (jnp.minimum((b * nch + (nch - 1 - c) + 1) * hb8, last8), 0)

    blk = lambda f: pl.BlockSpec((tc, RG_WIDTH), f)
    halo = lambda f: pl.BlockSpec((SUBLANES, RG_WIDTH), f)
    return pl.pallas_call(
        functools.partial(_rglru_kernel, n_chunks=nch),
        grid=(batch, nch),
        in_specs=[blk(main_f), halo(prev_f), halo(next_f), blk(main_b), halo(prev_b), halo(next_b),
                  _full(cw.shape), _full(cb.shape), _full(wgate.shape), _full(bgate.shape), _full(lam.shape)],
        out_specs=[blk(main_f), blk(main_b)],
        out_shape=[jax.ShapeDtypeStruct((t, RG_WIDTH), F32), jax.ShapeDtypeStruct((t, RG_WIDTH), F32)],
        scratch_shapes=[pltpu.VMEM((tc, RG_WIDTH), F32)] * 4 + [pltpu.VMEM((SUBLANES, RG_WIDTH), F32)] * 2,
        compiler_params=_cparams(2),
        name="rglru",
    )(rxg, rxg, rxg, rxg, rxg, rxg, cw, cb, wgate, bgate, lam)


def _attn_kernel(slopes_ref, q_ref, k_ref, v_ref, lq_ref, g_ref, o_ref, ka_scr, vt_scr, qa_scr, s_scr, p_scr, acc_scr,
                 *, seq, tq, tk, lam_init):
    h = pl.program_id(1)
    qi = pl.program_id(2)
    nk = seq // tk
    slope = slopes_ref[h]
    d2 = 2 * HEAD_DIM

    @pl.when(qi == 0)
    def _():
        ka_scr[:, 0:d2] = k_ref[...]
        pos = lax.broadcasted_iota(jnp.int32, (seq, d2), 0)
        col = lax.broadcasted_iota(jnp.int32, (seq, d2), 1)
        rem = (pos & (POS_BLOCK - 1)).astype(F32) * slope
        blk = (pos // POS_BLOCK * POS_BLOCK).astype(F32) * slope
        ext = jnp.where(col < 2, 1.0, jnp.where(col == 2, rem, jnp.where(col == 3, blk, 0.0)))
        ka_scr[:, d2:2 * d2] = ext.astype(BF16)
        ones_rows = jnp.where(lax.broadcasted_iota(jnp.int32, (VT_ROWS - d2, tk), 0) == 0, 1.0, 0.0).astype(BF16)
        for j in range(nk):
            vt_scr[j, 0:d2, :] = v_ref[j * tk:(j + 1) * tk, :].astype(F32).T.astype(BF16)
            vt_scr[j, d2:VT_ROWS, :] = ones_rows

    q = q_ref[...].astype(F32) * (HEAD_DIM ** -0.5)
    lane = lax.broadcasted_iota(jnp.int32, (tq, d2), 1)
    posq = qi * tq + lax.broadcasted_iota(jnp.int32, (tq, d2), 0)
    remq = (posq & (POS_BLOCK - 1)).astype(F32) * slope
    blkq = (posq // POS_BLOCK * POS_BLOCK).astype(F32) * slope
    extq_t = jnp.where(lane == 0, -remq, jnp.where(lane == 1, -blkq, jnp.where(lane < 4, 1.0, 0.0))).T
    for c in range(2):
        qc_t = jnp.where((lane >= c * HEAD_DIM) & (lane < (c + 1) * HEAD_DIM), q, 0.0).T.astype(BF16)
        for var in range(2):
            qa_scr[2 * var + c, 0:d2, :] = qc_t
            qa_scr[2 * var + c, d2:2 * d2, :] = (extq_t if var == 0 else -extq_t).astype(BF16)

    diag = (qi * tq) // tk

    def tile_of(i):
        t = i - 1
        return jnp.where(i == 0, diag, t + jnp.where(t >= diag, 1, 0))

    def keys(tile):
        return ka_scr[pl.ds(pl.multiple_of(tile * tk, tk), tk), :]

    def scores_into(slot, i):
        tile = tile_of(i)
        ka = keys(tile)
        var = jnp.where(tile > diag, 1, 0)
        for c in range(2):
            s_scr[slot, c] = _dot(ka, qa_scr[2 * var + c])

    def softmax_step(slot, c, m):
        s = s_scr[slot, c]
        mn = jnp.max(s, axis=0, keepdims=True)
        al = None
        if m is not None:
            mn = jnp.maximum(m, mn)
            al = jnp.exp(m - mn)
        p_scr[slot, c] = jnp.exp(s - mn).astype(BF16)
        return mn, al

    def finish(slot, i, al):
        vt = vt_scr[tile_of(i)]
        for c in range(2):
            acc_scr[c] = al[c] * acc_scr[c] + _dot(vt, p_scr[slot, c])

    def step(i, cur, m0, m1, al, last=False):
        nxt = 1 - cur
        finish(nxt, i - 1, al)
        m0, al0 = softmax_step(cur, 0, m0)
        m1, al1 = softmax_step(cur, 1, m1)
        if not last:
            scores_into(nxt, i + 1)
        return m0, m1, (al0, al1)

    kd = keys(diag)
    for c in range(2):
        s_scr[0, c] = jnp.minimum(_dot(kd, qa_scr[c]), _dot(kd, qa_scr[2 + c]))
    scores_into(1, 1)
    m0, _ = softmax_step(0, 0, None)
    m1, _ = softmax_step(0, 1, None)
    one = jnp.ones((1, tq), F32)
    acc_scr[...] = jnp.zeros_like(acc_scr)

    def body(j, carry):
        m0, m1, al0, al1 = carry
        al = (al0, al1)
        for u in range(ATT_UNROLL):
            m0, m1, al = step(1 + ATT_UNROLL * j + u, (1 + u) % 2, m0, m1, al)
        return (m0, m1, al[0], al[1])

    m0, m1, al0, al1 = lax.fori_loop(0, (nk - 2) // ATT_UNROLL, body, (m0, m1, one, one))
    m0, m1, al = step(nk - 1, 1, m0, m1, (al0, al1), last=True)
    finish(1, nk - 1, al)
    a0 = acc_scr[0]
    a1 = acc_scr[1]

    lq = lq_ref[...]
    lam = (jnp.exp(jnp.sum(lq[0:1, :] * lq[1:2, :], axis=1, keepdims=True))
           - jnp.exp(jnp.sum(lq[2:3, :] * lq[3:4, :], axis=1, keepdims=True)) + lam_init)
    o = a0[0:d2, :] / a0[d2:d2 + 1, :] - lam * (a1[0:d2, :] / a1[d2:d2 + 1, :])
    o = o * lax.rsqrt(jnp.mean(o * o, axis=0, keepdims=True) + NORM_EPS) * g_ref[...] * (1.0 - lam_init)
    o_ref[...] = o.T.astype(BF16)


def _attention(qkv, batch, seq, slopes, lq, g_col, lam_init):
    t = qkv.shape[0]
    tq, tk = ATT_TQ, ATT_TK
    nq = seq // tq
    kernel = functools.partial(_attn_kernel, seq=seq, tq=tq, tk=tk, lam_init=lam_init)
    return pl.pallas_call(
        kernel,
        grid=(batch, ATT_HEADS, nq),
        in_specs=[pl.BlockSpec(memory_space=pltpu.SMEM),
                  pl.BlockSpec((tq, 2 * HEAD_DIM), lambda b, h, i: (b * nq + i, h)),
                  pl.BlockSpec((seq, 2 * HEAD_DIM), lambda b, h, i: (b, ATT_HEADS + h)),
                  pl.BlockSpec((seq, 2 * HEAD_DIM), lambda b, h, i: (b, 2 * ATT_HEADS + h)),
                  _full(lq.shape), _full(g_col.shape)],
        out_specs=pl.BlockSpec((tq, 2 * HEAD_DIM), lambda b, h, i: (b * nq + i, h)),
        out_shape=jax.ShapeDtypeStruct((t, ATT_WIDTH), BF16),
        scratch_shapes=[pltpu.VMEM((seq, KA_WIDTH), BF16),
                        pltpu.VMEM((seq // tk, VT_ROWS, tk), BF16),
                        pltpu.VMEM((4, KA_WIDTH, tq), BF16),
                        pltpu.VMEM((2, 2, tk, tq), F32),
                        pltpu.VMEM((2, 2, tk, tq), BF16),
                        pltpu.VMEM((2, VT_ROWS, tq), F32)],
        compiler_params=_cparams(3),
        name="diff_attention",
    )(slopes, qkv, qkv, qkv, lq, g_col)


def _snap(v):
    r = round(v)
    return float(r) if abs(v - r) < 1e-12 else v


def _fourier_kernel(x_ref, fc_ref, fs_ref, twc_ref, tws_ref, cc_ref, sc_ref, o_ref, *, n2, scale):
    x = x_ref[...]
    ar = _dot(fc_ref[...], x)
    ai = _dot(fs_ref[...], x)
    c = twc_ref[...]
    s = tws_ref[...]
    br = ar * c - ai * s
    bi = -(ar * s + ai * c)
    cc = cc_ref[...]
    sc = sc_ref[...]
    for k2 in range(n2):
        xr = None
        xi = None
        for m in range(n2):
            ph = (m * k2) % n2
            cp = _snap(math.cos(2.0 * math.pi * ph / n2))
            sp = _snap(math.sin(2.0 * math.pi * ph / n2))
            brn = br[:, m * F_WIDTH:(m + 1) * F_WIDTH]
            bin_ = bi[:, m * F_WIDTH:(m + 1) * F_WIDTH]
            tr = brn * cp + bin_ * sp
            ti = bin_ * cp - brn * sp
            xr = tr if xr is None else xr + tr
            xi = ti if xi is None else xi + ti
        out = (_dot(xr.astype(BF16), cc) + _dot(xi.astype(BF16), sc)) * scale
        o_ref[k2] = out.astype(BF16)


def _fourier(fx, batch, seq, fc, fs, cc, sc):
    n1 = DFT_N1
    n2 = seq // n1
    tm = DFT_TM
    x2 = fx.reshape(batch, n1, n2 * F_WIDTH)
    k1 = lax.broadcasted_iota(jnp.int32, (n1, n2 * F_WIDTH), 0)
    m = lax.broadcasted_iota(jnp.int32, (n1, n2 * F_WIDTH), 1) // F_WIDTH
    ang = ((k1 * m) % seq).astype(F32) * (2.0 * math.pi / seq)
    twc = jnp.cos(ang)
    tws = jnp.sin(ang)
    kernel = functools.partial(_fourier_kernel, n2=n2, scale=1.0 / math.sqrt(seq * F_GW))
    out = pl.pallas_call(
        kernel,
        grid=(n1 // tm, batch),
        in_specs=[pl.BlockSpec((None, n1, n2 * F_WIDTH), lambda i, b: (b, 0, 0)),
                  pl.BlockSpec((tm, n1), lambda i, b: (i, 0)),
                  pl.BlockSpec((tm, n1), lambda i, b: (i, 0)),
                  pl.BlockSpec((tm, n2 * F_WIDTH), lambda i, b: (i, 0)),
                  pl.BlockSpec((tm, n2 * F_WIDTH), lambda i, b: (i, 0)),
                  _full(cc.shape), _full(sc.shape)],
        out_specs=pl.BlockSpec((None, n2, tm, F_WIDTH), lambda i, b: (b, 0, i, 0)),
        out_shape=jax.ShapeDtypeStruct((batch, n2, n1, F_WIDTH), BF16),
        compiler_params=_cparams(2),
        name="fourier",
    )(x2, fc, fs, twc, tws, cc, sc)
    return out.reshape(batch * seq, F_WIDTH)


def _block_diag(w):
    eye = jnp.eye(RG_HEADS, dtype=w.dtype)
    return (eye[:, None, :, None] * w[:, :, None, :]).reshape(RG_WIDTH, RG_WIDTH)


def _dft_tables():
    n = lax.broadcasted_iota(jnp.int32, (DFT_N1, DFT_N1), 0)
    k = lax.broadcasted_iota(jnp.int32, (DFT_N1, DFT_N1), 1)
    ang = ((n * k) % DFT_N1).astype(F32) * (2.0 * math.pi / DFT_N1)
    fc = jnp.cos(ang).astype(BF16)
    fs = jnp.sin(ang).astype(BF16)
    j = lax.broadcasted_iota(jnp.int32, (F_WIDTH, F_WIDTH), 0)
    jp = lax.broadcasted_iota(jnp.int32, (F_WIDTH, F_WIDTH), 1)
    same = (j // F_GW) == (jp // F_GW)
    ang_c = (((j % F_GW) * (jp % F_GW)) % F_GW).astype(F32) * (2.0 * math.pi / F_GW)
    cc = jnp.where(same, jnp.cos(ang_c), 0.0).astype(BF16)
    sc = jnp.where(same, jnp.sin(ang_c), 0.0).astype(BF16)
    return fc, fs, cc, sc


def _trunk(x, batch, seq, p, tables):
    fc, fs, cc, sc = tables
    slopes = 2.0 ** (-8.0 * jnp.arange(1, ATT_HEADS + 1, dtype=F32) / ATT_HEADS)
    for l in range(DEPTH):
        lam_init = 0.8 - 0.6 * math.exp(-0.3 * l)
        x1, rxg, qkv, fx = _ffn_inproj(x, p["ffn1_wg"][l], p["ffn1_wu"][l], p["ffn1_wd"][l],
                                       p["ln_g"][l, 0:1], p["ln_b"][l, 0:1], p["w_in"][l])
        hf, hb = _rglru(rxg, batch, seq, p["conv_w"][l], p["conv_b"][l], p["wgate"][l], p["bgate"][l],
                        p["rg_lambda"][l])
        yb = _attention(qkv, batch, seq, slopes, p["lambda_qk"][l], p["subln_g"][l], lam_init)
        yc = _fourier(fx, batch, seq, fc, fs, cc, sc)
        x = _outproj_ffn(x1, hf, hb, rxg, yb, yc, p["w_out"][l], p["ln_g"][l, 1:2], p["ln_b"][l, 1:2],
                         p["ffn2_wg"][l], p["ffn2_wu"][l], p["ffn2_wd"][l], p["ln_g"][l, 2:3], p["ln_b"][l, 2:3])
    return x


def kernel(x_prompt, x_sample, ln_g, ln_b, ffn1_wg, ffn1_wu, ffn1_wd, ffn2_wg, ffn2_wu, ffn2_wd, w_in, conv_w, conv_b,
           rg_wa, rg_ba, rg_wx, rg_bx, rg_lambda, lambda_qk, subln_g, w_out):
    wgate = jnp.stack([
        jnp.stack([jnp.concatenate([_block_diag(rg_wa[l, d]), _block_diag(rg_wx[l, d])], axis=1)
                   for d in range(2)]) for l in range(DEPTH)]).astype(BF16)
    bgate = jnp.concatenate([rg_ba, rg_bx], axis=-1)[:, :, None, :]
    p = dict(
        ln_g=ln_g, ln_b=ln_b,
        ffn1_wg=ffn1_wg.astype(BF16), ffn1_wu=ffn1_wu.astype(BF16), ffn1_wd=ffn1_wd.astype(BF16),
        ffn2_wg=ffn2_wg.astype(BF16), ffn2_wu=ffn2_wu.astype(BF16), ffn2_wd=ffn2_wd.astype(BF16),
        w_in=w_in.astype(BF16), w_out=w_out.astype(BF16),
        conv_w=conv_w, conv_b=conv_b[:, None, :],
        wgate=wgate, bgate=bgate, rg_lambda=rg_lambda[:, :, None, :],
        lambda_qk=lambda_qk, subln_g=subln_g[:, :, None],
    )
    tables = _dft_tables()
    outs = []
    for x in (x_prompt, x_sample):
        batch, seq, _ = x.shape
        y = _trunk(x.reshape(batch * seq, D_MODEL), batch, seq, p, tables)
        outs.append(y.reshape(batch, seq, D_MODEL))
    return tuple(outs)
```

```python
import functools
import math

import numpy as np
import jax
import jax.numpy as jnp
from jax import lax
from jax.experimental import pallas as pl
from jax.experimental.pallas import tpu as pltpu

D_MODEL = 1024
DEPTH = 2
D_FF = 2816
RG_WIDTH = 256
RG_HEADS = 4
RG_BW = 64
RG_C = 8.0
CONV_W = 4
ATT_HEADS = 4
HEAD_DIM = 64
ATT_WIDTH = 512
F_WIDTH = 256
F_GROUPS = 4
F_GW = 64
IN_WIDTH = 2304
ALPHA = (2.0 * DEPTH) ** 0.25
LN_EPS = 1e-5
NORM_EPS = 1e-5

BF16 = jnp.bfloat16
F32 = jnp.float32

VMEM_LIMIT_BYTES = 56 * 1024 * 1024
SUBLANES = 8

TOKEN_TILE = 512
FF_CHUNKS = ((0, 1024), (1024, 2048), (2048, 2816))
SCAN_CHUNK = 1024
ATT_TQ = 256
ATT_TK = 512
POS_BLOCK = 256
KA_WIDTH = 256
Q_PRESCALE = HEAD_DIM ** -0.5 * math.log2(math.e)
VT_ROWS = 144
DFT_N1 = 1024
DFT_TM = 256


def _cparams(n_axes):
    return pltpu.CompilerParams(
        dimension_semantics=("arbitrary",) * n_axes,
        vmem_limit_bytes=VMEM_LIMIT_BYTES,
    )


def _dot(a, b):
    return jnp.dot(a, b, preferred_element_type=F32)


def _layer_norm(y, g, b):
    mu = jnp.mean(y, axis=-1, keepdims=True)
    yc = y - mu
    var = jnp.mean(yc * yc, axis=-1, keepdims=True)
    return yc * lax.rsqrt(var + LN_EPS) * g + b


def _swiglu(xb, wg_ref, wu_ref, wd_ref):
    acc = None
    for lo, hi in FF_CHUNKS:
        hg = _dot(xb, wg_ref[:, lo:hi])
        hu = _dot(xb, wu_ref[:, lo:hi])
        h = (hg * jax.nn.sigmoid(hg) * hu).astype(BF16)
        part = _dot(h, wd_ref[lo:hi, :])
        acc = part if acc is None else acc + part
    return acc


def _ffn_inproj_kernel(x_ref, wg_ref, wu_ref, wd_ref, g_ref, b_ref, win_ref,
                       xo_ref, rxg_ref, qkv_ref, fx_ref):
    x = x_ref[...]
    y = ALPHA * x + 0.5 * _swiglu(x.astype(BF16), wg_ref, wu_ref, wd_ref)
    x1 = _layer_norm(y, g_ref[...], b_ref[...])
    xo_ref[...] = x1
    xb = x1.astype(BF16)
    rxg_ref[...] = _dot(xb, win_ref[:, 0:2 * RG_WIDTH])
    q_lo, k_lo, v_hi = 2 * RG_WIDTH, 2 * RG_WIDTH + ATT_WIDTH, 2 * RG_WIDTH + 3 * ATT_WIDTH
    qkv_ref[:, 0:ATT_WIDTH] = (_dot(xb, win_ref[:, q_lo:k_lo]) * Q_PRESCALE).astype(BF16)
    qkv_ref[:, ATT_WIDTH:3 * ATT_WIDTH] = _dot(xb, win_ref[:, k_lo:v_hi]).astype(BF16)
    fx_ref[...] = _dot(xb, win_ref[:, 2 * RG_WIDTH + 3 * ATT_WIDTH:IN_WIDTH]).astype(BF16)


def _full(shape):
    return pl.BlockSpec(shape, lambda *_: (0,) * len(shape))


def _ffn_inproj(x, wg, wu, wd, g, b, win):
    t = x.shape[0]
    tm = TOKEN_TILE
    row = lambda w: pl.BlockSpec((tm, w), lambda i: (i, 0))
    return pl.pallas_call(
        _ffn_inproj_kernel,
        grid=(t // tm,),
        in_specs=[row(D_MODEL), _full(wg.shape), _full(wu.shape), _full(wd.shape),
                  _full(g.shape), _full(b.shape), _full(win.shape)],
        out_specs=[row(D_MODEL), row(2 * RG_WIDTH), row(3 * ATT_WIDTH), row(F_WIDTH)],
        out_shape=[jax.ShapeDtypeStruct((t, D_MODEL), F32),
                   jax.ShapeDtypeStruct((t, 2 * RG_WIDTH), F32),
                   jax.ShapeDtypeStruct((t, 3 * ATT_WIDTH), BF16),
                   jax.ShapeDtypeStruct((t, F_WIDTH), BF16)],
        compiler_params=_cparams(1),
        name="ffn_inproj",
    )(x, wg, wu, wd, g, b, win)


def _outproj_ffn_kernel(x_ref, hf_ref, hb_ref, rg_ref, yb_ref, yc_ref, wo_ref, g1_ref, b1_ref,
                        wg_ref, wu_ref, wd_ref, g2_ref, b2_ref, o_ref):
    x = x_ref[...]
    ya = (jax.nn.gelu(rg_ref[...], approximate=True) * (hf_ref[...] + hb_ref[...])).astype(BF16)
    mix = (_dot(ya, wo_ref[0:RG_WIDTH, :])
           + _dot(yb_ref[...], wo_ref[RG_WIDTH:RG_WIDTH + ATT_WIDTH, :])
           + _dot(yc_ref[...], wo_ref[RG_WIDTH + ATT_WIDTH:D_MODEL, :]))
    x2 = _layer_norm(ALPHA * x + mix, g1_ref[...], b1_ref[...])
    y = ALPHA * x2 + 0.5 * _swiglu(x2.astype(BF16), wg_ref, wu_ref, wd_ref)
    o_ref[...] = _layer_norm(y, g2_ref[...], b2_ref[...])


def _outproj_ffn(x, hf, hb, rxg, yb, yc, wo, g1, b1, wg, wu, wd, g2, b2):
    t = x.shape[0]
    tm = TOKEN_TILE
    row = lambda w: pl.BlockSpec((tm, w), lambda i: (i, 0))
    rgate = pl.BlockSpec((tm, RG_WIDTH), lambda i: (i, 1))
    return pl.pallas_call(
        _outproj_ffn_kernel,
        grid=(t // tm,),
        in_specs=[row(D_MODEL), row(RG_WIDTH), row(RG_WIDTH), rgate, row(ATT_WIDTH), row(F_WIDTH),
                  _full(wo.shape), _full(g1.shape), _full(b1.shape),
                  _full(wg.shape), _full(wu.shape), _full(wd.shape), _full(g2.shape), _full(b2.shape)],
        out_specs=row(D_MODEL),
        out_shape=jax.ShapeDtypeStruct((t, D_MODEL), F32),
        compiler_params=_cparams(1),
        name="outproj_ffn",
    )(x, hf, hb, rxg, yb, yc, wo, g1, b1, wg, wu, wd, g2, b2)


def _rglru_direction(x_ref, prev_ref, next_ref, chunk, n_chunks, d, reverse,
                     cw_ref, cb_ref, wgate_ref, bgate_ref, lam_ref,
                     a_scr, u_scr, carry_scr, h_ref):
    tc = x_ref.shape[0]
    x = x_ref[...]
    prev = prev_ref[...] * jnp.where(chunk > 0, 1.0, 0.0)
    nxt = next_ref[...] * jnp.where(chunk < n_chunks - 1, 1.0, 0.0)
    row = lax.broadcasted_iota(jnp.int32, (tc, 1), 0)
    xm1 = jnp.where(row == 0, prev[7:8, :], pltpu.roll(x, 1, 0))
    xm2 = jnp.where(row == 0, prev[6:7, :], jnp.where(row == 1, prev[7:8, :], pltpu.roll(x, 2, 0)))
    xp1 = jnp.where(row == tc - 1, nxt[0:1, :], pltpu.roll(x, tc - 1, 0))
    cw = cw_ref[...]
    xc = cb_ref[...] + xm2 * cw[0:1, :] + xm1 * cw[1:2, :] + x * cw[2:3, :] + xp1 * cw[3:4, :]

    gates = _dot(xc.astype(BF16), wgate_ref[d]) + bgate_ref[d]
    r = jax.nn.sigmoid(gates[:, 0:RG_WIDTH])
    i = jax.nn.sigmoid(gates[:, RG_WIDTH:2 * RG_WIDTH])
    nlam = -lam_ref[d]
    softplus = jnp.maximum(nlam, 0.0) + jnp.log1p(jnp.exp(-jnp.abs(nlam)))
    log_a = (-RG_C * softplus) * r
    a = jnp.exp(log_a)
    th = jnp.tanh(log_a)
    u = jnp.sqrt(-2.0 * th / (1.0 - th)) * (i * xc)

    rig = row & (SUBLANES - 1)
    for k in (1, 2, 4):
        if reverse:
            keep = rig < SUBLANES - k
            shift = tc - k
        else:
            keep = rig >= k
            shift = k
        us = jnp.where(keep, pltpu.roll(u, shift, 0), 0.0)
        as_ = jnp.where(keep, pltpu.roll(a, shift, 0), 1.0)
        u = u + a * us
        a = a * as_
    a_scr[...] = a
    u_scr[...] = u

    n_groups = tc // SUBLANES

    def body(g, h):
        gi = n_groups - 1 - g if reverse else g
        r0 = pl.multiple_of(gi * SUBLANES, SUBLANES)
        hg = u_scr[pl.ds(r0, SUBLANES), :] + a_scr[pl.ds(r0, SUBLANES), :] * h
        h_ref[pl.ds(r0, SUBLANES), :] = hg
        edge = hg[0:1, :] if reverse else hg[SUBLANES - 1:SUBLANES, :]
        return jnp.broadcast_to(edge, (SUBLANES, RG_WIDTH))

    carry_scr[...] = lax.fori_loop(0, n_groups, body, carry_scr[...], unroll=8)


def _rglru_kernel(xf_ref, pf_ref, nf_ref, xb_ref, pb_ref, nb_ref,
                  cw_ref, cb_ref, wgate_ref, bgate_ref, lam_ref,
                  hf_ref, hb_ref, af_scr, uf_scr, ab_scr, ub_scr, cf_scr, cbk_scr, *, n_chunks):
    c = pl.program_id(1)

    @pl.when(c == 0)
    def _():
        cf_scr[...] = jnp.zeros_like(cf_scr)
        cbk_scr[...] = jnp.zeros_like(cbk_scr)

    _rglru_direction(xf_ref, pf_ref, nf_ref, c, n_chunks, 0, False,
                     cw_ref, cb_ref, wgate_ref, bgate_ref, lam_ref, af_scr, uf_scr, cf_scr, hf_ref)
    _rglru_direction(xb_ref, pb_ref, nb_ref, n_chunks - 1 - c, n_chunks, 1, True,
                     cw_ref, cb_ref, wgate_ref, bgate_ref, lam_ref, ab_scr, ub_scr, cbk_scr, hb_ref)


def _rglru(rxg, batch, seq, cw, cb, wgate, bgate, lam):
    t = rxg.shape[0]
    tc = SCAN_CHUNK
    nch = seq // tc
    hb8 = tc // SUBLANES
    last8 = t // SUBLANES - 1

    def main_f(b, c):
        return (b * nch + c, 0)

    def prev_f(b, c):
        return (jnp.maximum((b * nch + c) * hb8 - 1, 0), 0)

    def next_f(b, c):
        return (jnp.minimum((b * nch + c + 1) * hb8, last8), 0)

    def main_b(b, c):
        return (b * nch + (nch - 1 - c), 0)

    def prev_b(b, c):
        return (jnp.maximum((b * nch + (nch - 1 - c)) * hb8 - 1, 0), 0)

    def next_b(b, c):
        return (jnp.minimum((b * nch + (nch - 1 - c) + 1) * hb8, last8), 0)

    blk = lambda f: pl.BlockSpec((tc, RG_WIDTH), f)
    halo = lambda f: pl.BlockSpec((SUBLANES, RG_WIDTH), f)
    return pl.pallas_call(
        functools.partial(_rglru_kernel, n_chunks=nch),
        grid=(batch, nch),
        in_specs=[blk(main_f), halo(prev_f), halo(next_f), blk(main_b), halo(prev_b), halo(next_b),
                  _full(cw.shape), _full(cb.shape), _full(wgate.shape), _full(bgate.shape), _full(lam.shape)],
        out_specs=[blk(main_f), blk(main_b)],
        out_shape=[jax.ShapeDtypeStruct((t, RG_WIDTH), F32), jax.ShapeDtypeStruct((t, RG_WIDTH), F32)],
        scratch_shapes=[pltpu.VMEM((tc, RG_WIDTH), F32)] * 4 + [pltpu.VMEM((SUBLANES, RG_WIDTH), F32)] * 2,
        compiler_params=_cparams(2),
        name="rglru",
    )(rxg, rxg, rxg, rxg, rxg, rxg, cw, cb, wgate, bgate, lam)


def _bf16_split3(x):
    out = []
    for _ in range(3):
        hi = float(np.asarray(x, np.float32).astype(jnp.bfloat16).astype(np.float32))
        out.append(hi)
        x = x - hi
    return out


LOG2E_PARTS = _bf16_split3(math.log2(math.e))
EXT_COLS = 12


def _attn_kernel(slopes_ref, q_ref, k_ref, v_ref, lq_ref, g_ref, o_ref, ka_scr, vt_scr, qa_scr, s_scr, p_scr, acc_scr,
                 *, seq, tq, tk, lam_init):
    h = pl.program_id(1)
    nk = seq // tk
    nq = seq // tq
    slope = slopes_ref[h]
    d2 = 2 * HEAD_DIM
    l1, l2, l3 = LOG2E_PARTS

    def log2e_piece(idx):
        r = idx % 3
        return jnp.where(r == 0, l1, jnp.where(r == 1, l2, l3))

    ka_scr[:, 0:d2] = k_ref[...]
    pos = lax.broadcasted_iota(jnp.int32, (seq, d2), 0)
    col = lax.broadcasted_iota(jnp.int32, (seq, d2), 1)
    rem = pos & (POS_BLOCK - 1)
    ext = jnp.where(col < 6, log2e_piece(col),
                    jnp.where(col < 9, rem.astype(F32) * slope,
                              jnp.where(col < EXT_COLS, (pos - rem).astype(F32) * slope, 0.0)))
    ka_scr[:, d2:2 * d2] = ext.astype(BF16)
    ones_rows = jnp.where(lax.broadcasted_iota(jnp.int32, (VT_ROWS - d2, tk), 0) == 0, 1.0, 0.0).astype(BF16)
    for j in range(nk):
        vt_scr[j, 0:d2, :] = v_ref[j * tk:(j + 1) * tk, :].astype(F32).T.astype(BF16)
        vt_scr[j, d2:VT_ROWS, :] = ones_rows

    lq = lq_ref[...]
    lam = (jnp.exp(jnp.sum(lq[0:1, :] * lq[1:2, :], axis=1, keepdims=True))
           - jnp.exp(jnp.sum(lq[2:3, :] * lq[3:4, :], axis=1, keepdims=True)) + lam_init)
    rowi = lax.broadcasted_iota(jnp.int32, (d2, tq), 0)
    coli = lax.broadcasted_iota(jnp.int32, (d2, tq), 1)

    def q_tile(qi, carry):
        q0 = pl.multiple_of(qi * tq, tq)
        qt = q_ref[pl.ds(q0, tq), :].astype(F32).T
        posq = q0 + coli
        remq = posq & (POS_BLOCK - 1)
        ext_t = jnp.where(rowi < 3, -(remq.astype(F32) * slope),
                          jnp.where(rowi < 6, -((posq - remq).astype(F32) * slope),
                                    jnp.where(rowi < EXT_COLS, log2e_piece(rowi), 0.0)))
        for c in range(2):
            qc_t = jnp.where((rowi >= c * HEAD_DIM) & (rowi < (c + 1) * HEAD_DIM), qt, 0.0).astype(BF16)
            for var in range(2):
                qa_scr[2 * var + c, 0:d2, :] = qc_t
                qa_scr[2 * var + c, d2:2 * d2, :] = (ext_t if var == 0 else -ext_t).astype(BF16)

        diag = q0 // tk

        def tile_of(i):
            t = i - 1
            return jnp.where(i == 0, diag, t + jnp.where(t >= diag, 1, 0))

        def keys(tile):
            return ka_scr[pl.ds(pl.multiple_of(tile * tk, tk), tk), :]

        def scores_into(slot, i):
            tile = tile_of(i)
            ka = keys(tile)
            var = jnp.where(tile > diag, 1, 0)
            for c in range(2):
                s_scr[slot, c] = _dot(ka, qa_scr[2 * var + c])

        def softmax_step(slot, c, m):
            s = s_scr[slot, c]
            mn = jnp.max(s, axis=0, keepdims=True)
            al = None
            if m is not None:
                mn = jnp.maximum(m, mn)
                al = jnp.exp2(m - mn)
            p_scr[slot, c] = jnp.exp2(s - mn).astype(BF16)
            return mn, al

        def finish(slot, i, al):
            vt = vt_scr[tile_of(i)]
            for c in range(2):
                acc_scr[c] = al[c] * acc_scr[c] + _dot(vt, p_scr[slot, c])

        def step(i, cur, m0, m1, al, last=False):
            nxt = 1 - cur
            finish(nxt, i - 1, al)
            m0, al0 = softmax_step(cur, 0, m0)
            m1, al1 = softmax_step(cur, 1, m1)
            if not last:
                scores_into(nxt, i + 1)
            return m0, m1, (al0, al1)

        kd = keys(diag)
        for c in range(2):
            s_scr[0, c] = jnp.minimum(_dot(kd, qa_scr[c]), _dot(kd, qa_scr[2 + c]))
        scores_into(1, 1)
        m0, _ = softmax_step(0, 0, None)
        m1, _ = softmax_step(0, 1, None)
        acc_scr[...] = jnp.zeros_like(acc_scr)
        al = (jnp.ones((1, tq), F32),) * 2
        for i in range(1, nk):
            m0, m1, al = step(i, i % 2, m0, m1, al, last=(i == nk - 1))
        finish(1, nk - 1, al)
        a0 = acc_scr[0]
        a1 = acc_scr[1]
        o = a0[0:d2, :] / a0[d2:d2 + 1, :] - lam * (a1[0:d2, :] / a1[d2:d2 + 1, :])
        o = o * lax.rsqrt(jnp.mean(o * o, axis=0, keepdims=True) + NORM_EPS) * g_ref[...] * (1.0 - lam_init)
        o_ref[pl.ds(q0, tq), :] = o.T.astype(BF16)
        return carry

    lax.fori_loop(0, nq, q_tile, 0)


def _attention(qkv, batch, seq, slopes, lq, g_col, lam_init):
    t = qkv.shape[0]
    tq, tk = ATT_TQ, ATT_TK
    kernel = functools.partial(_attn_kernel, seq=seq, tq=tq, tk=tk, lam_init=lam_init)
    col = lambda off: pl.BlockSpec((seq, 2 * HEAD_DIM), lambda b, h: (b, off + h))
    return pl.pallas_call(
        kernel,
        grid=(batch, ATT_HEADS),
        in_specs=[pl.BlockSpec(memory_space=pltpu.SMEM), col(0), col(ATT_HEADS), col(2 * ATT_HEADS),
                  _full(lq.shape), _full(g_col.shape)],
        out_specs=col(0),
        out_shape=jax.ShapeDtypeStruct((t, ATT_WIDTH), BF16),
        scratch_shapes=[pltpu.VMEM((seq, KA_WIDTH), BF16),
                        pltpu.VMEM((seq // tk, VT_ROWS, tk), BF16),
                        pltpu.VMEM((4, KA_WIDTH, tq), BF16),
                        pltpu.VMEM((2, 2, tk, tq), F32),
                        pltpu.VMEM((2, 2, tk, tq), BF16),
                        pltpu.VMEM((2, VT_ROWS, tq), F32)],
        compiler_params=_cparams(2),
        name="diff_attention",
    )(slopes, qkv, qkv, qkv, lq, g_col)


def _snap(v):
    r = round(v)
    return float(r) if abs(v - r) < 1e-12 else v


def _fourier_kernel(x_ref, fc_ref, fs_ref, twc_ref, tws_ref, cc_ref, sc_ref, o_ref, *, n2, scale):
    x = x_ref[...]
    ar = _dot(fc_ref[...], x)
    ai = _dot(fs_ref[...], x)
    c = twc_ref[...]
    s = tws_ref[...]
    br = ar * c - ai * s
    bi = -(ar * s + ai * c)
    cc = cc_ref[...]
    sc = sc_ref[...]
    for k2 in range(n2):
        xr = None
        xi = None
        for m in range(n2):
            ph = (m * k2) % n2
            cp = _snap(math.cos(2.0 * math.pi * ph / n2))
            sp = _snap(math.sin(2.0 * math.pi * ph / n2))
            brn = br[:, m * F_WIDTH:(m + 1) * F_WIDTH]
            bin_ = bi[:, m * F_WIDTH:(m + 1) * F_WIDTH]
            tr = brn * cp + bin_ * sp
            ti = bin_ * cp - brn * sp
            xr = tr if xr is None else xr + tr
            xi = ti if xi is None else xi + ti
        out = (_dot(xr.astype(BF16), cc) + _dot(xi.astype(BF16), sc)) * scale
        o_ref[k2] = out.astype(BF16)


def _fourier(fx, batch, seq, fc, fs, cc, sc):
    n1 = DFT_N1
    n2 = seq // n1
    tm = DFT_TM
    x2 = fx.reshape(batch, n1, n2 * F_WIDTH)
    k1 = lax.broadcasted_iota(jnp.int32, (n1, n2 * F_WIDTH), 0)
    m = lax.broadcasted_iota(jnp.int32, (n1, n2 * F_WIDTH), 1) // F_WIDTH
    ang = ((k1 * m) % seq).astype(F32) * (2.0 * math.pi / seq)
    twc = jnp.cos(ang)
    tws = jnp.sin(ang)
    kernel = functools.partial(_fourier_kernel, n2=n2, scale=1.0 / math.sqrt(seq * F_GW))
    out = pl.pallas_call(
        kernel,
        grid=(n1 // tm, batch),
        in_specs=[pl.BlockSpec((None, n1, n2 * F_WIDTH), lambda i, b: (b, 0, 0)),
                  pl.BlockSpec((tm, n1), lambda i, b: (i, 0)),
                  pl.BlockSpec((tm, n1), lambda i, b: (i, 0)),
                  pl.BlockSpec((tm, n2 * F_WIDTH), lambda i, b: (i, 0)),
                  pl.BlockSpec((tm, n2 * F_WIDTH), lambda i, b: (i, 0)),
                  _full(cc.shape), _full(sc.shape)],
        out_specs=pl.BlockSpec((None, n2, tm, F_WIDTH), lambda i, b: (b, 0, i, 0)),
        out_shape=jax.ShapeDtypeStruct((batch, n2, n1, F_WIDTH), BF16),
        compiler_params=_cparams(2),
        name="fourier",
    )(x2, fc, fs, twc, tws, cc, sc)
    return out.reshape(batch * seq, F_WIDTH)


def _block_diag(w):
    eye = jnp.eye(RG_HEADS, dtype=w.dtype)
    return (eye[:, None, :, None] * w[:, :, None, :]).reshape(RG_WIDTH, RG_WIDTH)


def _dft_tables():
    n = lax.broadcasted_iota(jnp.int32, (DFT_N1, DFT_N1), 0)
    k = lax.broadcasted_iota(jnp.int32, (DFT_N1, DFT_N1), 1)
    ang = ((n * k) % DFT_N1).astype(F32) * (2.0 * math.pi / DFT_N1)
    fc = jnp.cos(ang).astype(BF16)
    fs = jnp.sin(ang).astype(BF16)
    j = lax.broadcasted_iota(jnp.int32, (F_WIDTH, F_WIDTH), 0)
    jp = lax.broadcasted_iota(jnp.int32, (F_WIDTH, F_WIDTH), 1)
    same = (j // F_GW) == (jp // F_GW)
    ang_c = (((j % F_GW) * (jp % F_GW)) % F_GW).astype(F32) * (2.0 * math.pi / F_GW)
    cc = jnp.where(same, jnp.cos(ang_c), 0.0).astype(BF16)
    sc = jnp.where(same, jnp.sin(ang_c), 0.0).astype(BF16)
    return fc, fs, cc, sc


def _trunk(x, batch, seq, p, tables):
    fc, fs, cc, sc = tables
    slopes = 2.0 ** (-8.0 * jnp.arange(1, ATT_HEADS + 1, dtype=F32) / ATT_HEADS)
    for l in range(DEPTH):
        lam_init = 0.8 - 0.6 * math.exp(-0.3 * l)
        x1, rxg, qkv, fx = _ffn_inproj(x, p["ffn1_wg"][l], p["ffn1_wu"][l], p["ffn1_wd"][l],
                                       p["ln_g"][l, 0:1], p["ln_b"][l, 0:1], p["w_in"][l])
        hf, hb = _rglru(rxg, batch, seq, p["conv_w"][l], p["conv_b"][l], p["wgate"][l], p["bgate"][l],
                        p["rg_lambda"][l])
        yb = _attention(qkv, batch, seq, slopes, p["lambda_qk"][l], p["subln_g"][l], lam_init)
        yc = _fourier(fx, batch, seq, fc, fs, cc, sc)
        x = _outproj_ffn(x1, hf, hb, rxg, yb, yc, p["w_out"][l], p["ln_g"][l, 1:2], p["ln_b"][l, 1:2],
                         p["ffn2_wg"][l], p["ffn2_wu"][l], p["ffn2_wd"][l], p["ln_g"][l, 2:3], p["ln_b"][l, 2:3])
    return x


def kernel(x_prompt, x_sample, ln_g, ln_b, ffn1_wg, ffn1_wu, ffn1_wd, ffn2_wg, ffn2_wu, ffn2_wd, w_in, conv_w, conv_b,
           rg_wa, rg_ba, rg_wx, rg_bx, rg_lambda, lambda_qk, subln_g, w_out):
    wgate = jnp.stack([
        jnp.stack([jnp.concatenate([_block_diag(rg_wa[l, d]), _block_diag(rg_wx[l, d])], axis=1)
                   for d in range(2)]) for l in range(DEPTH)]).astype(BF16)
    bgate = jnp.concatenate([rg_ba, rg_bx], axis=-1)[:, :, None, :]
    p = dict(
        ln_g=ln_g, ln_b=ln_b,
        ffn1_wg=ffn1_wg.astype(BF16), ffn1_wu=ffn1_wu.astype(BF16), ffn1_wd=ffn1_wd.astype(BF16),
        ffn2_wg=ffn2_wg.astype(BF16), ffn2_wu=ffn2_wu.astype(BF16), ffn2_wd=ffn2_wd.astype(BF16),
        w_in=w_in.astype(BF16), w_out=w_out.astype(BF16),
        conv_w=conv_w, conv_b=conv_b[:, None, :],
        wgate=wgate, bgate=bgate, rg_lambda=rg_lambda[:, :, None, :],
        lambda_qk=lambda_qk, subln_g=subln_g[:, :, None],
    )
    tables = _dft_tables()
    outs = []
    for x in (x_prompt, x_sample):
        batch, seq, _ = x.shape
        y = _trunk(x.reshape(batch * seq, D_MODEL), batch, seq, p, tables)
        outs.append(y.reshape(batch, seq, D_MODEL))
    return tuple(outs)
```

```python
import functools
import math

import numpy as np
import jax
import jax.numpy as jnp
from jax import lax
from jax.experimental import pallas as pl
from jax.experimental.pallas import tpu as pltpu

D_MODEL = 1024
DEPTH = 2
D_FF = 2816
RG_WIDTH = 256
RG_HEADS = 4
RG_BW = 64
RG_C = 8.0
CONV_W = 4
ATT_HEADS = 4
HEAD_DIM = 64
ATT_WIDTH = 512
F_WIDTH = 256
F_GROUPS = 4
F_GW = 64
IN_WIDTH = 2304
ALPHA = (2.0 * DEPTH) ** 0.25
LN_EPS = 1e-5
NORM_EPS = 1e-5

BF16 = jnp.bfloat16
F32 = jnp.float32

VMEM_LIMIT_BYTES = 56 * 1024 * 1024
SUBLANES = 8

TOKEN_TILE = 512
FF_CHUNKS = ((0, 1024), (1024, 2048), (2048, 2816))
SCAN_CHUNK = 1024
ATT_TQ = 256
ATT_TK = 512
ATT_PREFETCH_MIN_STEPS = 16
POS_BLOCK = 256
KA_WIDTH = 256
Q_PRESCALE = HEAD_DIM ** -0.5 * math.log2(math.e)
VT_ROWS = 144
DFT_N1 = 1024
DFT_TM = 256


def _cparams(n_axes):
    return pltpu.CompilerParams(
        dimension_semantics=("arbitrary",) * n_axes,
        vmem_limit_bytes=VMEM_LIMIT_BYTES,
    )


def _dot(a, b):
    return jnp.dot(a, b, preferred_element_type=F32)


def _layer_norm(y, g, b):
    mu = jnp.mean(y, axis=-1, keepdims=True)
    yc = y - mu
    var = jnp.mean(yc * yc, axis=-1, keepdims=True)
    return yc * lax.rsqrt(var + LN_EPS) * g + b


def _swiglu(xb, wg_ref, wu_ref, wd_ref):
    acc = None
    for lo, hi in FF_CHUNKS:
        hg = _dot(xb, wg_ref[:, lo:hi])
        hu = _dot(xb, wu_ref[:, lo:hi])
        h = (hg * jax.nn.sigmoid(hg) * hu).astype(BF16)
        part = _dot(h, wd_ref[lo:hi, :])
        acc = part if acc is None else acc + part
    return acc


def _ffn_inproj_kernel(x_ref, wg_ref, wu_ref, wd_ref, g_ref, b_ref, win_ref,
                       xo_ref, rxg_ref, qkv_ref, fx_ref):
    x = x_ref[...]
    y = ALPHA * x + 0.5 * _swiglu(x.astype(BF16), wg_ref, wu_ref, wd_ref)
    x1 = _layer_norm(y, g_ref[...], b_ref[...])
    xo_ref[...] = x1
    xb = x1.astype(BF16)
    rxg_ref[...] = _dot(xb, win_ref[:, 0:2 * RG_WIDTH])
    q_lo, k_lo, v_hi = 2 * RG_WIDTH, 2 * RG_WIDTH + ATT_WIDTH, 2 * RG_WIDTH + 3 * ATT_WIDTH
    qkv_ref[:, 0:ATT_WIDTH] = (_dot(xb, win_ref[:, q_lo:k_lo]) * Q_PRESCALE).astype(BF16)
    qkv_ref[:, ATT_WIDTH:3 * ATT_WIDTH] = _dot(xb, win_ref[:, k_lo:v_hi]).astype(BF16)
    fx_ref[...] = _dot(xb, win_ref[:, 2 * RG_WIDTH + 3 * ATT_WIDTH:IN_WIDTH]).astype(BF16)


def _full(shape):
    return pl.BlockSpec(shape, lambda *_: (0,) * len(shape))


def _ffn_inproj(x, wg, wu, wd, g, b, win):
    t = x.shape[0]
    tm = TOKEN_TILE
    row = lambda w: pl.BlockSpec((tm, w), lambda i: (i, 0))
    return pl.pallas_call(
        _ffn_inproj_kernel,
        grid=(t // tm,),
        in_specs=[row(D_MODEL), _full(wg.shape), _full(wu.shape), _full(wd.shape),
                  _full(g.shape), _full(b.shape), _full(win.shape)],
        out_specs=[row(D_MODEL), row(2 * RG_WIDTH), row(3 * ATT_WIDTH), row(F_WIDTH)],
        out_shape=[jax.ShapeDtypeStruct((t, D_MODEL), F32),
                   jax.ShapeDtypeStruct((t, 2 * RG_WIDTH), F32),
                   jax.ShapeDtypeStruct((t, 3 * ATT_WIDTH), BF16),
                   jax.ShapeDtypeStruct((t, F_WIDTH), BF16)],
        compiler_params=_cparams(1),
        name="ffn_inproj",
    )(x, wg, wu, wd, g, b, win)


def _outproj_ffn_kernel(x_ref, hf_ref, hb_ref, rg_ref, yb_ref, yc_ref, wo_ref, g1_ref, b1_ref,
                        wg_ref, wu_ref, wd_ref, g2_ref, b2_ref, o_ref):
    x = x_ref[...]
    ya = (jax.nn.gelu(rg_ref[...], approximate=True) * (hf_ref[...] + hb_ref[...])).astype(BF16)
    mix = (_dot(ya, wo_ref[0:RG_WIDTH, :])
           + _dot(yb_ref[...], wo_ref[RG_WIDTH:RG_WIDTH + ATT_WIDTH, :])
           + _dot(yc_ref[...], wo_ref[RG_WIDTH + ATT_WIDTH:D_MODEL, :]))
    x2 = _layer_norm(ALPHA * x + mix, g1_ref[...], b1_ref[...])
    y = ALPHA * x2 + 0.5 * _swiglu(x2.astype(BF16), wg_ref, wu_ref, wd_ref)
    o_ref[...] = _layer_norm(y, g2_ref[...], b2_ref[...])


def _outproj_ffn(x, hf, hb, rxg, yb, yc, wo, g1, b1, wg, wu, wd, g2, b2):
    t = x.shape[0]
    tm = TOKEN_TILE
    row = lambda w: pl.BlockSpec((tm, w), lambda i: (i, 0))
    rgate = pl.BlockSpec((tm, RG_WIDTH), lambda i: (i, 1))
    return pl.pallas_call(
        _outproj_ffn_kernel,
        grid=(t // tm,),
        in_specs=[row(D_MODEL), row(RG_WIDTH), row(RG_WIDTH), rgate, row(ATT_WIDTH), row(F_WIDTH),
                  _full(wo.shape), _full(g1.shape), _full(b1.shape),
                  _full(wg.shape), _full(wu.shape), _full(wd.shape), _full(g2.shape), _full(b2.shape)],
        out_specs=row(D_MODEL),
        out_shape=jax.ShapeDtypeStruct((t, D_MODEL), F32),
        compiler_params=_cparams(1),
        name="outproj_ffn",
    )(x, hf, hb, rxg, yb, yc, wo, g1, b1, wg, wu, wd, g2, b2)


def _rglru_direction(x_ref, prev_ref, next_ref, chunk, n_chunks, d, reverse,
                     cw_ref, cb_ref, wgate_ref, bgate_ref, lam_ref,
                     a_scr, u_scr, carry_scr, h_ref):
    tc = x_ref.shape[0]
    n_groups = tc // SUBLANES
    grp = (n_groups, SUBLANES, RG_WIDTH)
    rig = lax.broadcasted_iota(jnp.int32, (1, SUBLANES, 1), 1)
    x = x_ref[...].reshape(grp)
    prev = (prev_ref[...] * jnp.where(chunk > 0, 1.0, 0.0)).reshape(1, SUBLANES, RG_WIDTH)
    nxt = (next_ref[...] * jnp.where(chunk < n_chunks - 1, 1.0, 0.0)).reshape(1, SUBLANES, RG_WIDTH)

    def delayed(k):
        y = pltpu.roll(x, k, 1)
        before = jnp.concatenate([pltpu.roll(prev, k, 1), y[:-1]], axis=0)
        return jnp.where(rig >= k, y, before)

    ahead = pltpu.roll(x, SUBLANES - 1, 1)
    after = jnp.concatenate([ahead[1:], pltpu.roll(nxt, SUBLANES - 1, 1)], axis=0)
    xp1 = jnp.where(rig < SUBLANES - 1, ahead, after)
    cw = cw_ref[...]
    xc = cb_ref[...] + delayed(2) * cw[0:1, :] + delayed(1) * cw[1:2, :] + x * cw[2:3, :] + xp1 * cw[3:4, :]
    xc = xc.reshape(tc, RG_WIDTH)

    gates = _dot(xc.astype(BF16), wgate_ref[d]) + bgate_ref[d]
    r = jax.nn.sigmoid(gates[:, 0:RG_WIDTH])
    i = jax.nn.sigmoid(gates[:, RG_WIDTH:2 * RG_WIDTH])
    nlam = -lam_ref[d]
    softplus = jnp.maximum(nlam, 0.0) + jnp.log1p(jnp.exp(-jnp.abs(nlam)))
    log_a = (-RG_C * softplus) * r
    a = jnp.exp(log_a)
    th = jnp.tanh(log_a)
    u = jnp.sqrt(-2.0 * th / (1.0 - th)) * (i * xc)

    u = u.reshape(grp)
    a = a.reshape(grp)
    for k in (1, 2, 4):
        if reverse:
            keep = rig < SUBLANES - k
            shift = SUBLANES - k
        else:
            keep = rig >= k
            shift = k
        us = jnp.where(keep, pltpu.roll(u, shift, 1), 0.0)
        as_ = jnp.where(keep, pltpu.roll(a, shift, 1), 1.0)
        u = u + a * us
        a = a * as_
    a_scr[...] = a.reshape(tc, RG_WIDTH)
    u_scr[...] = u.reshape(tc, RG_WIDTH)


    def body(g, h):
        gi = n_groups - 1 - g if reverse else g
        r0 = pl.multiple_of(gi * SUBLANES, SUBLANES)
        hg = u_scr[pl.ds(r0, SUBLANES), :] + a_scr[pl.ds(r0, SUBLANES), :] * h
        h_ref[pl.ds(r0, SUBLANES), :] = hg
        edge = hg[0:1, :] if reverse else hg[SUBLANES - 1:SUBLANES, :]
        return jnp.broadcast_to(edge, (SUBLANES, RG_WIDTH))

    carry_scr[...] = lax.fori_loop(0, n_groups, body, carry_scr[...], unroll=8)


def _rglru_kernel(xf_ref, pf_ref, nf_ref, xb_ref, pb_ref, nb_ref,
                  cw_ref, cb_ref, wgate_ref, bgate_ref, lam_ref,
                  hf_ref, hb_ref, af_scr, uf_scr, ab_scr, ub_scr, cf_scr, cbk_scr, *, n_chunks):
    c = pl.program_id(1)

    @pl.when(c == 0)
    def _():
        cf_scr[...] = jnp.zeros_like(cf_scr)
        cbk_scr[...] = jnp.zeros_like(cbk_scr)

    _rglru_direction(xf_ref, pf_ref, nf_ref, c, n_chunks, 0, False,
                     cw_ref, cb_ref, wgate_ref, bgate_ref, lam_ref, af_scr, uf_scr, cf_scr, hf_ref)
    _rglru_direction(xb_ref, pb_ref, nb_ref, n_chunks - 1 - c, n_chunks, 1, True,
                     cw_ref, cb_ref, wgate_ref, bgate_ref, lam_ref, ab_scr, ub_scr, cbk_scr, hb_ref)


def _rglru(rxg, batch, seq, cw, cb, wgate, bgate, lam):
    t = rxg.shape[0]
    tc = SCAN_CHUNK
    nch = seq // tc
    hb8 = tc // SUBLANES
    last8 = t // SUBLANES - 1

    def main_f(b, c):
        return (b * nch + c, 0)

    def prev_f(b, c):
        return (jnp.maximum((b * nch + c) * hb8 - 1, 0), 0)

    def next_f(b, c):
        return (jnp.minimum((b * nch + c + 1) * hb8, last8), 0)

    def main_b(b, c):
        return (b * nch + (nch - 1 - c), 0)

    def prev_b(b, c):
        return (jnp.maximum((b * nch + (nch - 1 - c)) * hb8 - 1, 0), 0)

    def next_b(b, c):
        return (jnp.minimum((b * nch + (nch - 1 - c) + 1) * hb8, last8), 0)

    blk = lambda f: pl.BlockSpec((tc, RG_WIDTH), f)
    halo = lambda f: pl.BlockSpec((SUBLANES, RG_WIDTH), f)
    return pl.pallas_call(
        functools.partial(_rglru_kernel, n_chunks=nch),
        grid=(batch, nch),
        in_specs=[blk(main_f), halo(prev_f), halo(next_f), blk(main_b), halo(prev_b), halo(next_b),
                  _full(cw.shape), _full(cb.shape), _full(wgate.shape), _full(bgate.shape), _full(lam.shape)],
        out_specs=[blk(main_f), blk(main_b)],
        out_shape=[jax.ShapeDtypeStruct((t, RG_WIDTH), F32), jax.ShapeDtypeStruct((t, RG_WIDTH), F32)],
        scratch_shapes=[pltpu.VMEM((tc, RG_WIDTH), F32)] * 4 + [pltpu.VMEM((SUBLANES, RG_WIDTH), F32)] * 2,
        compiler_params=_cparams(2),
        name="rglru",
    )(rxg, rxg, rxg, rxg, rxg, rxg, cw, cb, wgate, bgate, lam)


def _bf16_split3(x):
    out = []
    for _ in range(3):
        hi = float(np.asarray(x, np.float32).astype(jnp.bfloat16).astype(np.float32))
        out.append(hi)
        x = x - hi
    return out


LOG2E_PARTS = _bf16_split3(math.log2(math.e))
EXT_COLS = 12


def _attn_kernel(slopes_ref, q_ref, k_ref, v_ref, lq_ref, g_ref, o_ref, ka_scr, vt_scr, qa_scr, s_scr, p_scr, acc_scr,
                 *, seq, tq, tk, lam_init):
    h = pl.program_id(1)
    nk = seq // tk
    nq = seq // tq
    prefetch_head = nk >= ATT_PREFETCH_MIN_STEPS
    slope = slopes_ref[h]
    d2 = 2 * HEAD_DIM
    l1, l2, l3 = LOG2E_PARTS

    def log2e_piece(idx):
        r = idx % 3
        return jnp.where(r == 0, l1, jnp.where(r == 1, l2, l3))

    ka_scr[:, 0:d2] = k_ref[...]
    pos = lax.broadcasted_iota(jnp.int32, (seq, d2), 0)
    col = lax.broadcasted_iota(jnp.int32, (seq, d2), 1)
    rem = pos & (POS_BLOCK - 1)
    ext = jnp.where(col < 6, log2e_piece(col),
                    jnp.where(col < 9, rem.astype(F32) * slope,
                              jnp.where(col < EXT_COLS, (pos - rem).astype(F32) * slope, 0.0)))
    ka_scr[:, d2:2 * d2] = ext.astype(BF16)
    ones_rows = jnp.where(lax.broadcasted_iota(jnp.int32, (VT_ROWS - d2, tk), 0) == 0, 1.0, 0.0).astype(BF16)
    for j in range(nk):
        vt_scr[j, 0:d2, :] = v_ref[j * tk:(j + 1) * tk, :].astype(F32).T.astype(BF16)
        vt_scr[j, d2:VT_ROWS, :] = ones_rows

    lq = lq_ref[...]
    lam = (jnp.exp(jnp.sum(lq[0:1, :] * lq[1:2, :], axis=1, keepdims=True))
           - jnp.exp(jnp.sum(lq[2:3, :] * lq[3:4, :], axis=1, keepdims=True)) + lam_init)
    rowi = lax.broadcasted_iota(jnp.int32, (d2, tq), 0)
    coli = lax.broadcasted_iota(jnp.int32, (d2, tq), 1)

    def tile_of(i, diag):
        t = i - 1
        return jnp.where(i == 0, diag, t + jnp.where(t >= diag, 1, 0))

    def keys(tile):
        return ka_scr[pl.ds(pl.multiple_of(tile * tk, tk), tk), :]

    def scores_into(slot, i, diag):
        tile = tile_of(i, diag)
        ka = keys(tile)
        var = jnp.where(tile > diag, 1, 0)
        for c in range(2):
            s_scr[slot, c] = _dot(ka, qa_scr[2 * var + c])

    def head(qi):
        q0 = pl.multiple_of(qi * tq, tq)
        qt = q_ref[pl.ds(q0, tq), :].astype(F32).T
        posq = q0 + coli
        remq = posq & (POS_BLOCK - 1)
        ext_t = jnp.where(rowi < 3, -(remq.astype(F32) * slope),
                          jnp.where(rowi < 6, -((posq - remq).astype(F32) * slope),
                                    jnp.where(rowi < EXT_COLS, log2e_piece(rowi), 0.0)))
        for c in range(2):
            qc_t = jnp.where((rowi >= c * HEAD_DIM) & (rowi < (c + 1) * HEAD_DIM), qt, 0.0).astype(BF16)
            for var in range(2):
                qa_scr[2 * var + c, 0:d2, :] = qc_t
                qa_scr[2 * var + c, d2:2 * d2, :] = (ext_t if var == 0 else -ext_t).astype(BF16)
        diag = q0 // tk
        kd = keys(diag)
        for c in range(2):
            s_scr[0, c] = jnp.minimum(_dot(kd, qa_scr[c]), _dot(kd, qa_scr[2 + c]))
        scores_into(1, 1, diag)

    def q_tile(qi):
        q0 = pl.multiple_of(qi * tq, tq)
        diag = q0 // tk
        if not prefetch_head:
            head(qi)

        def softmax_step(slot, c, m):
            s = s_scr[slot, c]
            mn = jnp.max(s, axis=0, keepdims=True)
            al = None
            if m is not None:
                mn = jnp.maximum(m, mn)
                al = jnp.exp2(m - mn)
            p_scr[slot, c] = jnp.exp2(s - mn).astype(BF16)
            return mn, al

        def finish(slot, i, al):
            vt = vt_scr[tile_of(i, diag)]
            for c in range(2):
                acc_scr[c] = al[c] * acc_scr[c] + _dot(vt, p_scr[slot, c])

        def step(i, cur, m0, m1, al, last=False):
            nxt = 1 - cur
            finish(nxt, i - 1, al)
            m0, al0 = softmax_step(cur, 0, m0)
            m1, al1 = softmax_step(cur, 1, m1)
            if not last:
                scores_into(nxt, i + 1, diag)
            return m0, m1, (al0, al1)

        m0, _ = softmax_step(0, 0, None)
        m1, _ = softmax_step(0, 1, None)
        acc_scr[...] = jnp.zeros_like(acc_scr)
        al = (jnp.ones((1, tq), F32),) * 2
        for i in range(1, nk):
            m0, m1, al = step(i, i % 2, m0, m1, al, last=(i == nk - 1))
        finish(1, nk - 1, al)
        if prefetch_head:
            head(jnp.minimum(qi + 1, nq - 1))
        a0 = acc_scr[0]
        a1 = acc_scr[1]
        o = a0[0:d2, :] / a0[d2:d2 + 1, :] - lam * (a1[0:d2, :] / a1[d2:d2 + 1, :])
        o = o * lax.rsqrt(jnp.mean(o * o, axis=0, keepdims=True) + NORM_EPS) * g_ref[...] * (1.0 - lam_init)
        o_ref[pl.ds(q0, tq), :] = o.T.astype(BF16)

    if prefetch_head:
        head(0)

    def q_loop(qi, carry):
        q_tile(qi)
        return carry

    lax.fori_loop(0, nq, q_loop, 0)


def _attention(qkv, batch, seq, slopes, lq, g_col, lam_init):
    t = qkv.shape[0]
    tq, tk = ATT_TQ, ATT_TK
    kernel = functools.partial(_attn_kernel, seq=seq, tq=tq, tk=tk, lam_init=lam_init)
    col = lambda off: pl.BlockSpec((seq, 2 * HEAD_DIM), lambda b, h: (b, off + h))
    return pl.pallas_call(
        kernel,
        grid=(batch, ATT_HEADS),
        in_specs=[pl.BlockSpec(memory_space=pltpu.SMEM), col(0), col(ATT_HEADS), col(2 * ATT_HEADS),
                  _full(lq.shape), _full(g_col.shape)],
        out_specs=col(0),
        out_shape=jax.ShapeDtypeStruct((t, ATT_WIDTH), BF16),
        scratch_shapes=[pltpu.VMEM((seq, KA_WIDTH), BF16),
                        pltpu.VMEM((seq // tk, VT_ROWS, tk), BF16),
                        pltpu.VMEM((4, KA_WIDTH, tq), BF16),
                        pltpu.VMEM((2, 2, tk, tq), F32),
                        pltpu.VMEM((2, 2, tk, tq), BF16),
                        pltpu.VMEM((2, VT_ROWS, tq), F32)],
        compiler_params=_cparams(2),
        name="diff_attention",
    )(slopes, qkv, qkv, qkv, lq, g_col)


def _snap(v):
    r = round(v)
    return float(r) if abs(v - r) < 1e-12 else v


def _fourier_kernel(x_ref, fc_ref, fs_ref, twc_ref, tws_ref, cc_ref, sc_ref, o_ref, *, n2, scale):
    x = x_ref[...]
    ar = _dot(fc_ref[...], x)
    ai = _dot(fs_ref[...], x)
    c = twc_ref[...]
    s = tws_ref[...]
    br = ar * c - ai * s
    bi = -(ar * s + ai * c)
    cc = cc_ref[...]
    sc = sc_ref[...]
    for k2 in range(n2):
        xr = None
        xi = None
        for m in range(n2):
            ph = (m * k2) % n2
            cp = _snap(math.cos(2.0 * math.pi * ph / n2))
            sp = _snap(math.sin(2.0 * math.pi * ph / n2))
            brn = br[:, m * F_WIDTH:(m + 1) * F_WIDTH]
            bin_ = bi[:, m * F_WIDTH:(m + 1) * F_WIDTH]
            tr = brn * cp + bin_ * sp
            ti = bin_ * cp - brn * sp
            xr = tr if xr is None else xr + tr
            xi = ti if xi is None else xi + ti
        out = (_dot(xr.astype(BF16), cc) + _dot(xi.astype(BF16), sc)) * scale
        o_ref[k2] = out.astype(BF16)


def _fourier(fx, batch, seq, fc, fs, cc, sc):
    n1 = DFT_N1
    n2 = seq // n1
    tm = DFT_TM
    x2 = fx.reshape(batch, n1, n2 * F_WIDTH)
    k1 = lax.broadcasted_iota(jnp.int32, (n1, n2 * F_WIDTH), 0)
    m = lax.broadcasted_iota(jnp.int32, (n1, n2 * F_WIDTH), 1) // F_WIDTH
    ang = ((k1 * m) % seq).astype(F32) * (2.0 * math.pi / seq)
    twc = jnp.cos(ang)
    tws = jnp.sin(ang)
    kernel = functools.partial(_fourier_kernel, n2=n2, scale=1.0 / math.sqrt(seq * F_GW))
    out = pl.pallas_call(
        kernel,
        grid=(n1 // tm, batch),
        in_specs=[pl.BlockSpec((None, n1, n2 * F_WIDTH), lambda i, b: (b, 0, 0)),
                  pl.BlockSpec((tm, n1), lambda i, b: (i, 0)),
                  pl.BlockSpec((tm, n1), lambda i, b: (i, 0)),
                  pl.BlockSpec((tm, n2 * F_WIDTH), lambda i, b: (i, 0)),
                  pl.BlockSpec((tm, n2 * F_WIDTH), lambda i, b: (i, 0)),
                  _full(cc.shape), _full(sc.shape)],
        out_specs=pl.BlockSpec((None, n2, tm, F_WIDTH), lambda i, b: (b, 0, i, 0)),
        out_shape=jax.ShapeDtypeStruct((batch, n2, n1, F_WIDTH), BF16),
        compiler_params=_cparams(2),
        name="fourier",
    )(x2, fc, fs, twc, tws, cc, sc)
    return out.reshape(batch * seq, F_WIDTH)


def _block_diag(w):
    eye = jnp.eye(RG_HEADS, dtype=w.dtype)
    return (eye[:, None, :, None] * w[:, :, None, :]).reshape(RG_WIDTH, RG_WIDTH)


def _dft_tables():
    n = lax.broadcasted_iota(jnp.int32, (DFT_N1, DFT_N1), 0)
    k = lax.broadcasted_iota(jnp.int32, (DFT_N1, DFT_N1), 1)
    ang = ((n * k) % DFT_N1).astype(F32) * (2.0 * math.pi / DFT_N1)
    fc = jnp.cos(ang).astype(BF16)
    fs = jnp.sin(ang).astype(BF16)
    j = lax.broadcasted_iota(jnp.int32, (F_WIDTH, F_WIDTH), 0)
    jp = lax.broadcasted_iota(jnp.int32, (F_WIDTH, F_WIDTH), 1)
    same = (j // F_GW) == (jp // F_GW)
    ang_c = (((j % F_GW) * (jp % F_GW)) % F_GW).astype(F32) * (2.0 * math.pi / F_GW)
    cc = jnp.where(same, jnp.cos(ang_c), 0.0).astype(BF16)
    sc = jnp.where(same, jnp.sin(ang_c), 0.0).astype(BF16)
    return fc, fs, cc, sc


def _trunk(x, batch, seq, p, tables):
    fc, fs, cc, sc = tables
    slopes = 2.0 ** (-8.0 * jnp.arange(1, ATT_HEADS + 1, dtype=F32) / ATT_HEADS)
    for l in range(DEPTH):
        lam_init = 0.8 - 0.6 * math.exp(-0.3 * l)
        x1, rxg, qkv, fx = _ffn_inproj(x, p["ffn1_wg"][l], p["ffn1_wu"][l], p["ffn1_wd"][l],
                                       p["ln_g"][l, 0:1], p["ln_b"][l, 0:1], p["w_in"][l])
        hf, hb = _rglru(rxg, batch, seq, p["conv_w"][l], p["conv_b"][l], p["wgate"][l], p["bgate"][l],
                        p["rg_lambda"][l])
        yb = _attention(qkv, batch, seq, slopes, p["lambda_qk"][l], p["subln_g"][l], lam_init)
        yc = _fourier(fx, batch, seq, fc, fs, cc, sc)
        x = _outproj_ffn(x1, hf, hb, rxg, yb, yc, p["w_out"][l], p["ln_g"][l, 1:2], p["ln_b"][l, 1:2],
                         p["ffn2_wg"][l], p["ffn2_wu"][l], p["ffn2_wd"][l], p["ln_g"][l, 2:3], p["ln_b"][l, 2:3])
    return x


def kernel(x_prompt, x_sample, ln_g, ln_b, ffn1_wg, ffn1_wu, ffn1_wd, ffn2_wg, ffn2_wu, ffn2_wd, w_in, conv_w, conv_b,
           rg_wa, rg_ba, rg_wx, rg_bx, rg_lambda, lambda_qk, subln_g, w_out):
    wgate = jnp.stack([
        jnp.stack([jnp.concatenate([_block_diag(rg_wa[l, d]), _block_diag(rg_wx[l, d])], axis=1)
                   for d in range(2)]) for l in range(DEPTH)]).astype(BF16)
    bgate = jnp.concatenate([rg_ba, rg_bx], axis=-1)[:, :, None, :]
    p = dict(
        ln_g=ln_g, ln_b=ln_b,
        ffn1_wg=ffn1_wg.astype(BF16), ffn1_wu=ffn1_wu.astype(BF16), ffn1_wd=ffn1_wd.astype(BF16),
        ffn2_wg=ffn2_wg.astype(BF16), ffn2_wu=ffn2_wu.astype(BF16), ffn2_wd=ffn2_wd.astype(BF16),
        w_in=w_in.astype(BF16), w_out=w_out.astype(BF16),
        conv_w=conv_w, conv_b=conv_b[:, None, :],
        wgate=wgate, bgate=bgate, rg_lambda=rg_lambda[:, :, None, :],
        lambda_qk=lambda_qk, subln_g=subln_g[:, :, None],
    )
    tables = _dft_tables()
    outs = []
    for x in (x_prompt, x_sample):
        batch, seq, _ = x.shape
        y = _trunk(x.reshape(batch * seq, D_MODEL), batch, seq, p, tables)
        outs.append(y.reshape(batch, seq, D_MODEL))
    return tuple(outs)
```

```python
import functools
import math

import numpy as np
import jax
import jax.numpy as jnp
from jax import lax
from jax.experimental import pallas as pl
from jax.experimental.pallas import tpu as pltpu

D_MODEL = 1024
DEPTH = 2
D_FF = 2816
RG_WIDTH = 256
RG_HEADS = 4
RG_BW = 64
RG_C = 8.0
CONV_W = 4
ATT_HEADS = 4
HEAD_DIM = 64
ATT_WIDTH = 512
F_WIDTH = 256
F_GROUPS = 4
F_GW = 64
IN_WIDTH = 2304
ALPHA = (2.0 * DEPTH) ** 0.25
LN_EPS = 1e-5
NORM_EPS = 1e-5

BF16 = jnp.bfloat16
F32 = jnp.float32

VMEM_LIMIT_BYTES = 56 * 1024 * 1024
SUBLANES = 8

TOKEN_TILE = 512
FF_CHUNKS = ((0, 1024), (1024, 2048), (2048, 2816))
SCAN_CHUNK = 1024
ATT_TQ = 256
ATT_TQ_PLAIN = 512
ATT_TK = 512
ATT_PREFETCH_MIN_STEPS = 16
POS_BLOCK = 256
KA_WIDTH = 256
Q_PRESCALE = HEAD_DIM ** -0.5 * math.log2(math.e)
VT_ROWS = 144
DFT_N1 = 1024
DFT_TM = 256


def _cparams(n_axes):
    return pltpu.CompilerParams(
        dimension_semantics=("arbitrary",) * n_axes,
        vmem_limit_bytes=VMEM_LIMIT_BYTES,
    )


def _dot(a, b):
    return jnp.dot(a, b, preferred_element_type=F32)


def _layer_norm(y, g, b):
    mu = jnp.mean(y, axis=-1, keepdims=True)
    yc = y - mu
    var = jnp.mean(yc * yc, axis=-1, keepdims=True)
    return yc * lax.rsqrt(var + LN_EPS) * g + b


def _swiglu(xb, wg_ref, wu_ref, wd_ref):
    acc = None
    for lo, hi in FF_CHUNKS:
        hg = _dot(xb, wg_ref[:, lo:hi])
        hu = _dot(xb, wu_ref[:, lo:hi])
        h = (hg * jax.nn.sigmoid(hg) * hu).astype(BF16)
        part = _dot(h, wd_ref[lo:hi, :])
        acc = part if acc is None else acc + part
    return acc


def _ffn_inproj_kernel(x_ref, wg_ref, wu_ref, wd_ref, g_ref, b_ref, win_ref,
                       xo_ref, rxg_ref, qkv_ref, fx_ref):
    x = x_ref[...]
    y = ALPHA * x + 0.5 * _swiglu(x.astype(BF16), wg_ref, wu_ref, wd_ref)
    x1 = _layer_norm(y, g_ref[...], b_ref[...])
    xo_ref[...] = x1
    xb = x1.astype(BF16)
    rxg_ref[...] = _dot(xb, win_ref[:, 0:2 * RG_WIDTH])
    q_lo, k_lo, v_hi = 2 * RG_WIDTH, 2 * RG_WIDTH + ATT_WIDTH, 2 * RG_WIDTH + 3 * ATT_WIDTH
    qkv_ref[:, 0:ATT_WIDTH] = (_dot(xb, win_ref[:, q_lo:k_lo]) * Q_PRESCALE).astype(BF16)
    qkv_ref[:, ATT_WIDTH:3 * ATT_WIDTH] = _dot(xb, win_ref[:, k_lo:v_hi]).astype(BF16)
    fx_ref[...] = _dot(xb, win_ref[:, 2 * RG_WIDTH + 3 * ATT_WIDTH:IN_WIDTH]).astype(BF16)


def _full(shape):
    return pl.BlockSpec(shape, lambda *_: (0,) * len(shape))


def _ffn_inproj(x, wg, wu, wd, g, b, win):
    t = x.shape[0]
    tm = TOKEN_TILE
    row = lambda w: pl.BlockSpec((tm, w), lambda i: (i, 0))
    return pl.pallas_call(
        _ffn_inproj_kernel,
        grid=(t // tm,),
        in_specs=[row(D_MODEL), _full(wg.shape), _full(wu.shape), _full(wd.shape),
                  _full(g.shape), _full(b.shape), _full(win.shape)],
        out_specs=[row(D_MODEL), row(2 * RG_WIDTH), row(3 * ATT_WIDTH), row(F_WIDTH)],
        out_shape=[jax.ShapeDtypeStruct((t, D_MODEL), F32),
                   jax.ShapeDtypeStruct((t, 2 * RG_WIDTH), F32),
                   jax.ShapeDtypeStruct((t, 3 * ATT_WIDTH), BF16),
                   jax.ShapeDtypeStruct((t, F_WIDTH), BF16)],
        compiler_params=_cparams(1),
        name="ffn_inproj",
    )(x, wg, wu, wd, g, b, win)


def _outproj_ffn_kernel(x_ref, hf_ref, hb_ref, rg_ref, yb_ref, yc_ref, wo_ref, g1_ref, b1_ref,
                        wg_ref, wu_ref, wd_ref, g2_ref, b2_ref, o_ref):
    x = x_ref[...]
    ya = (jax.nn.gelu(rg_ref[...], approximate=True) * (hf_ref[...] + hb_ref[...])).astype(BF16)
    mix = (_dot(ya, wo_ref[0:RG_WIDTH, :])
           + _dot(yb_ref[...], wo_ref[RG_WIDTH:RG_WIDTH + ATT_WIDTH, :])
           + _dot(yc_ref[...], wo_ref[RG_WIDTH + ATT_WIDTH:D_MODEL, :]))
    x2 = _layer_norm(ALPHA * x + mix, g1_ref[...], b1_ref[...])
    y = ALPHA * x2 + 0.5 * _swiglu(x2.astype(BF16), wg_ref, wu_ref, wd_ref)
    o_ref[...] = _layer_norm(y, g2_ref[...], b2_ref[...])


def _outproj_ffn(x, hf, hb, rxg, yb, yc, wo, g1, b1, wg, wu, wd, g2, b2):
    t = x.shape[0]
    tm = TOKEN_TILE
    row = lambda w: pl.BlockSpec((tm, w), lambda i: (i, 0))
    rgate = pl.BlockSpec((tm, RG_WIDTH), lambda i: (i, 1))
    return pl.pallas_call(
        _outproj_ffn_kernel,
        grid=(t // tm,),
        in_specs=[row(D_MODEL), row(RG_WIDTH), row(RG_WIDTH), rgate, row(ATT_WIDTH), row(F_WIDTH),
                  _full(wo.shape), _full(g1.shape), _full(b1.shape),
                  _full(wg.shape), _full(wu.shape), _full(wd.shape), _full(g2.shape), _full(b2.shape)],
        out_specs=row(D_MODEL),
        out_shape=jax.ShapeDtypeStruct((t, D_MODEL), F32),
        compiler_params=_cparams(1),
        name="outproj_ffn",
    )(x, hf, hb, rxg, yb, yc, wo, g1, b1, wg, wu, wd, g2, b2)


def _rglru_direction(x_ref, prev_ref, next_ref, chunk, n_chunks, d, reverse,
                     cw_ref, cb_ref, wgate_ref, bgate_ref, lam_ref,
                     a_scr, u_scr, carry_scr, h_ref):
    tc = x_ref.shape[0]
    n_groups = tc // SUBLANES
    grp = (n_groups, SUBLANES, RG_WIDTH)
    rig = lax.broadcasted_iota(jnp.int32, (1, SUBLANES, 1), 1)
    x = x_ref[...].reshape(grp)
    prev = (prev_ref[...] * jnp.where(chunk > 0, 1.0, 0.0)).reshape(1, SUBLANES, RG_WIDTH)
    nxt = (next_ref[...] * jnp.where(chunk < n_chunks - 1, 1.0, 0.0)).reshape(1, SUBLANES, RG_WIDTH)

    def delayed(k):
        y = pltpu.roll(x, k, 1)
        before = jnp.concatenate([pltpu.roll(prev, k, 1), y[:-1]], axis=0)
        return jnp.where(rig >= k, y, before)

    ahead = pltpu.roll(x, SUBLANES - 1, 1)
    after = jnp.concatenate([ahead[1:], pltpu.roll(nxt, SUBLANES - 1, 1)], axis=0)
    xp1 = jnp.where(rig < SUBLANES - 1, ahead, after)
    cw = cw_ref[...]
    xc = cb_ref[...] + delayed(2) * cw[0:1, :] + delayed(1) * cw[1:2, :] + x * cw[2:3, :] + xp1 * cw[3:4, :]
    xc = xc.reshape(tc, RG_WIDTH)

    gates = _dot(xc.astype(BF16), wgate_ref[d]) + bgate_ref[d]
    r = jax.nn.sigmoid(gates[:, 0:RG_WIDTH])
    i = jax.nn.sigmoid(gates[:, RG_WIDTH:2 * RG_WIDTH])
    nlam = -lam_ref[d]
    softplus = jnp.maximum(nlam, 0.0) + jnp.log1p(jnp.exp(-jnp.abs(nlam)))
    log_a = (-RG_C * softplus) * r
    a = jnp.exp(log_a)
    th = jnp.tanh(log_a)
    u = jnp.sqrt(-2.0 * th / (1.0 - th)) * (i * xc)

    u = u.reshape(grp)
    a = a.reshape(grp)
    for k in (1, 2, 4):
        if reverse:
            keep = rig < SUBLANES - k
            shift = SUBLANES - k
        else:
            keep = rig >= k
            shift = k
        us = jnp.where(keep, pltpu.roll(u, shift, 1), 0.0)
        as_ = jnp.where(keep, pltpu.roll(a, shift, 1), 1.0)
        u = u + a * us
        a = a * as_
    a_scr[...] = a.reshape(tc, RG_WIDTH)
    u_scr[...] = u.reshape(tc, RG_WIDTH)


    def body(g, h):
        gi = n_groups - 1 - g if reverse else g
        r0 = pl.multiple_of(gi * SUBLANES, SUBLANES)
        hg = u_scr[pl.ds(r0, SUBLANES), :] + a_scr[pl.ds(r0, SUBLANES), :] * h
        h_ref[pl.ds(r0, SUBLANES), :] = hg
        edge = hg[0:1, :] if reverse else hg[SUBLANES - 1:SUBLANES, :]
        return jnp.broadcast_to(edge, (SUBLANES, RG_WIDTH))

    carry_scr[...] = lax.fori_loop(0, n_groups, body, carry_scr[...], unroll=8)


def _rglru_kernel(xf_ref, pf_ref, nf_ref, xb_ref, pb_ref, nb_ref,
                  cw_ref, cb_ref, wgate_ref, bgate_ref, lam_ref,
                  hf_ref, hb_ref, af_scr, uf_scr, ab_scr, ub_scr, cf_scr, cbk_scr, *, n_chunks):
    c = pl.program_id(1)

    @pl.when(c == 0)
    def _():
        cf_scr[...] = jnp.zeros_like(cf_scr)
        cbk_scr[...] = jnp.zeros_like(cbk_scr)

    _rglru_direction(xf_ref, pf_ref, nf_ref, c, n_chunks, 0, False,
                     cw_ref, cb_ref, wgate_ref, bgate_ref, lam_ref, af_scr, uf_scr, cf_scr, hf_ref)
    _rglru_direction(xb_ref, pb_ref, nb_ref, n_chunks - 1 - c, n_chunks, 1, True,
                     cw_ref, cb_ref, wgate_ref, bgate_ref, lam_ref, ab_scr, ub_scr, cbk_scr, hb_ref)


def _rglru(rxg, batch, seq, cw, cb, wgate, bgate, lam):
    t = rxg.shape[0]
    tc = SCAN_CHUNK
    nch = seq // tc
    hb8 = tc // SUBLANES
    last8 = t // SUBLANES - 1

    def main_f(b, c):
        return (b * nch + c, 0)

    def prev_f(b, c):
        return (jnp.maximum((b * nch + c) * hb8 - 1, 0), 0)

    def next_f(b, c):
        return (jnp.minimum((b * nch + c + 1) * hb8, last8), 0)

    def main_b(b, c):
        return (b * nch + (nch - 1 - c), 0)

    def prev_b(b, c):
        return (jnp.maximum((b * nch + (nch - 1 - c)) * hb8 - 1, 0), 0)

    def next_b(b, c):
        return (jnp.minimum((b * nch + (nch - 1 - c) + 1) * hb8, last8), 0)

    blk = lambda f: pl.BlockSpec((tc, RG_WIDTH), f)
    halo = lambda f: pl.BlockSpec((SUBLANES, RG_WIDTH), f)
    return pl.pallas_call(
        functools.partial(_rglru_kernel, n_chunks=nch),
        grid=(batch, nch),
        in_specs=[blk(main_f), halo(prev_f), halo(next_f), blk(main_b), halo(prev_b), halo(next_b),
                  _full(cw.shape), _full(cb.shape), _full(wgate.shape), _full(bgate.shape), _full(lam.shape)],
        out_specs=[blk(main_f), blk(main_b)],
        out_shape=[jax.ShapeDtypeStruct((t, RG_WIDTH), F32), jax.ShapeDtypeStruct((t, RG_WIDTH), F32)],
        scratch_shapes=[pltpu.VMEM((tc, RG_WIDTH), F32)] * 4 + [pltpu.VMEM((SUBLANES, RG_WIDTH), F32)] * 2,
        compiler_params=_cparams(2),
        name="rglru",
    )(rxg, rxg, rxg, rxg, rxg, rxg, cw, cb, wgate, bgate, lam)


def _bf16_split3(x):
    out = []
    for _ in range(3):
        hi = float(np.asarray(x, np.float32).astype(jnp.bfloat16).astype(np.float32))
        out.append(hi)
        x = x - hi
    return out


LOG2E_PARTS = _bf16_split3(math.log2(math.e))
EXT_COLS = 12


def _attn_kernel(slopes_ref, q_ref, k_ref, v_ref, lq_ref, g_ref, o_ref, ka_scr, vt_scr, qa_scr, s_scr, p_scr, acc_scr,
                 *, seq, tq, tk, lam_init):
    h = pl.program_id(1)
    nk = seq // tk
    nq = seq // tq
    prefetch_head = nk >= ATT_PREFETCH_MIN_STEPS
    slope = slopes_ref[h]
    d2 = 2 * HEAD_DIM
    l1, l2, l3 = LOG2E_PARTS

    def log2e_piece(idx):
        r = idx % 3
        return jnp.where(r == 0, l1, jnp.where(r == 1, l2, l3))

    ka_scr[:, 0:d2] = k_ref[...]
    pos = lax.broadcasted_iota(jnp.int32, (seq, d2), 0)
    col = lax.broadcasted_iota(jnp.int32, (seq, d2), 1)
    rem = pos & (POS_BLOCK - 1)
    ext = jnp.where(col < 6, log2e_piece(col),
                    jnp.where(col < 9, rem.astype(F32) * slope,
                              jnp.where(col < EXT_COLS, (pos - rem).astype(F32) * slope, 0.0)))
    ka_scr[:, d2:2 * d2] = ext.astype(BF16)
    ones_rows = jnp.where(lax.broadcasted_iota(jnp.int32, (VT_ROWS - d2, tk), 0) == 0, 1.0, 0.0).astype(BF16)
    for j in range(nk):
        vt_scr[j, 0:d2, :] = v_ref[j * tk:(j + 1) * tk, :].astype(F32).T.astype(BF16)
        vt_scr[j, d2:VT_ROWS, :] = ones_rows

    lq = lq_ref[...]
    lam = (jnp.exp(jnp.sum(lq[0:1, :] * lq[1:2, :], axis=1, keepdims=True))
           - jnp.exp(jnp.sum(lq[2:3, :] * lq[3:4, :], axis=1, keepdims=True)) + lam_init)
    rowi = lax.broadcasted_iota(jnp.int32, (d2, tq), 0)
    coli = lax.broadcasted_iota(jnp.int32, (d2, tq), 1)

    def tile_of(i, diag):
        t = i - 1
        return jnp.where(i == 0, diag, t + jnp.where(t >= diag, 1, 0))

    def keys(tile):
        return ka_scr[pl.ds(pl.multiple_of(tile * tk, tk), tk), :]

    def scores_into(slot, i, diag):
        tile = tile_of(i, diag)
        ka = keys(tile)
        var = jnp.where(tile > diag, 1, 0)
        for c in range(2):
            s_scr[slot, c] = _dot(ka, qa_scr[2 * var + c])

    def head(qi):
        q0 = pl.multiple_of(qi * tq, tq)
        qt = q_ref[pl.ds(q0, tq), :].astype(F32).T
        posq = q0 + coli
        remq = posq & (POS_BLOCK - 1)
        ext_t = jnp.where(rowi < 3, -(remq.astype(F32) * slope),
                          jnp.where(rowi < 6, -((posq - remq).astype(F32) * slope),
                                    jnp.where(rowi < EXT_COLS, log2e_piece(rowi), 0.0)))
        for c in range(2):
            qc_t = jnp.where((rowi >= c * HEAD_DIM) & (rowi < (c + 1) * HEAD_DIM), qt, 0.0).astype(BF16)
            for var in range(2):
                qa_scr[2 * var + c, 0:d2, :] = qc_t
                qa_scr[2 * var + c, d2:2 * d2, :] = (ext_t if var == 0 else -ext_t).astype(BF16)
        diag = q0 // tk
        kd = keys(diag)
        for c in range(2):
            s_scr[0, c] = jnp.minimum(_dot(kd, qa_scr[c]), _dot(kd, qa_scr[2 + c]))
        scores_into(1, 1, diag)

    def q_tile(qi):
        q0 = pl.multiple_of(qi * tq, tq)
        diag = q0 // tk
        if not prefetch_head:
            head(qi)

        def softmax_step(slot, c, m):
            s = s_scr[slot, c]
            mn = jnp.max(s, axis=0, keepdims=True)
            al = None
            if m is not None:
                mn = jnp.maximum(m, mn)
                al = jnp.exp2(m - mn)
            p_scr[slot, c] = jnp.exp2(s - mn).astype(BF16)
            return mn, al

        def finish(slot, i, al):
            vt = vt_scr[tile_of(i, diag)]
            for c in range(2):
                acc_scr[c] = al[c] * acc_scr[c] + _dot(vt, p_scr[slot, c])

        def step(i, cur, m0, m1, al, last=False):
            nxt = 1 - cur
            finish(nxt, i - 1, al)
            m0, al0 = softmax_step(cur, 0, m0)
            m1, al1 = softmax_step(cur, 1, m1)
            if not last:
                scores_into(nxt, i + 1, diag)
            return m0, m1, (al0, al1)

        m0, _ = softmax_step(0, 0, None)
        m1, _ = softmax_step(0, 1, None)
        acc_scr[...] = jnp.zeros_like(acc_scr)
        al = (jnp.ones((1, tq), F32),) * 2
        for i in range(1, nk):
            m0, m1, al = step(i, i % 2, m0, m1, al, last=(i == nk - 1))
        finish(1, nk - 1, al)
        if prefetch_head:
            head(jnp.minimum(qi + 1, nq - 1))
        a0 = acc_scr[0]
        a1 = acc_scr[1]
        o = a0[0:d2, :] / a0[d2:d2 + 1, :] - lam * (a1[0:d2, :] / a1[d2:d2 + 1, :])
        o = o * lax.rsqrt(jnp.mean(o * o, axis=0, keepdims=True) + NORM_EPS) * g_ref[...] * (1.0 - lam_init)
        o_ref[pl.ds(q0, tq), :] = o.T.astype(BF16)

    if prefetch_head:
        head(0)

    def q_loop(qi, carry):
        q_tile(qi)
        return carry

    lax.fori_loop(0, nq, q_loop, 0)


def _attention(qkv, batch, seq, slopes, lq, g_col, lam_init):
    t = qkv.shape[0]
    tk = ATT_TK
    tq = ATT_TQ if seq // tk >= ATT_PREFETCH_MIN_STEPS else ATT_TQ_PLAIN
    kernel = functools.partial(_attn_kernel, seq=seq, tq=tq, tk=tk, lam_init=lam_init)
    col = lambda off: pl.BlockSpec((seq, 2 * HEAD_DIM), lambda b, h: (b, off + h))
    return pl.pallas_call(
        kernel,
        grid=(batch, ATT_HEADS),
        in_specs=[pl.BlockSpec(memory_space=pltpu.SMEM), col(0), col(ATT_HEADS), col(2 * ATT_HEADS),
                  _full(lq.shape), _full(g_col.shape)],
        out_specs=col(0),
        out_shape=jax.ShapeDtypeStruct((t, ATT_WIDTH), BF16),
        scratch_shapes=[pltpu.VMEM((seq, KA_WIDTH), BF16),
                        pltpu.VMEM((seq // tk, VT_ROWS, tk), BF16),
                        pltpu.VMEM((4, KA_WIDTH, tq), BF16),
                        pltpu.VMEM((2, 2, tk, tq), F32),
                        pltpu.VMEM((2, 2, tk, tq), BF16),
                        pltpu.VMEM((2, VT_ROWS, tq), F32)],
        compiler_params=_cparams(2),
        name="diff_attention",
    )(slopes, qkv, qkv, qkv, lq, g_col)


def _snap(v):
    r = round(v)
    return float(r) if abs(v - r) < 1e-12 else v


def _fourier_kernel(x_ref, fc_ref, fs_ref, twc_ref, tws_ref, cc_ref, sc_ref, o_ref, *, n2, scale):
    x = x_ref[...]
    ar = _dot(fc_ref[...], x)
    ai = _dot(fs_ref[...], x)
    c = twc_ref[...]
    s = tws_ref[...]
    br = ar * c - ai * s
    bi = -(ar * s + ai * c)
    cc = cc_ref[...]
    sc = sc_ref[...]
    for k2 in range(n2):
        xr = None
        xi = None
        for m in range(n2):
            ph = (m * k2) % n2
            cp = _snap(math.cos(2.0 * math.pi * ph / n2))
            sp = _snap(math.sin(2.0 * math.pi * ph / n2))
            brn = br[:, m * F_WIDTH:(m + 1) * F_WIDTH]
            bin_ = bi[:, m * F_WIDTH:(m + 1) * F_WIDTH]
            tr = brn * cp + bin_ * sp
            ti = bin_ * cp - brn * sp
            xr = tr if xr is None else xr + tr
            xi = ti if xi is None else xi + ti
        out = (_dot(xr.astype(BF16), cc) + _dot(xi.astype(BF16), sc)) * scale
        o_ref[k2] = out.astype(BF16)


def _fourier(fx, batch, seq, fc, fs, cc, sc):
    n1 = DFT_N1
    n2 = seq // n1
    tm = DFT_TM
    x2 = fx.reshape(batch, n1, n2 * F_WIDTH)
    k1 = lax.broadcasted_iota(jnp.int32, (n1, n2 * F_WIDTH), 0)
    m = lax.broadcasted_iota(jnp.int32, (n1, n2 * F_WIDTH), 1) // F_WIDTH
    ang = ((k1 * m) % seq).astype(F32) * (2.0 * math.pi / seq)
    twc = jnp.cos(ang)
    tws = jnp.sin(ang)
    kernel = functools.partial(_fourier_kernel, n2=n2, scale=1.0 / math.sqrt(seq * F_GW))
    out = pl.pallas_call(
        kernel,
        grid=(n1 // tm, batch),
        in_specs=[pl.BlockSpec((None, n1, n2 * F_WIDTH), lambda i, b: (b, 0, 0)),
                  pl.BlockSpec((tm, n1), lambda i, b: (i, 0)),
                  pl.BlockSpec((tm, n1), lambda i, b: (i, 0)),
                  pl.BlockSpec((tm, n2 * F_WIDTH), lambda i, b: (i, 0)),
                  pl.BlockSpec((tm, n2 * F_WIDTH), lambda i, b: (i, 0)),
                  _full(cc.shape), _full(sc.shape)],
        out_specs=pl.BlockSpec((None, n2, tm, F_WIDTH), lambda i, b: (b, 0, i, 0)),
        out_shape=jax.ShapeDtypeStruct((batch, n2, n1, F_WIDTH), BF16),
        compiler_params=_cparams(2),
        name="fourier",
    )(x2, fc, fs, twc, tws, cc, sc)
    return out.reshape(batch * seq, F_WIDTH)


def _block_diag(w):
    eye = jnp.eye(RG_HEADS, dtype=w.dtype)
    return (eye[:, None, :, None] * w[:, :, None, :]).reshape(RG_WIDTH, RG_WIDTH)


def _dft_tables():
    n = lax.broadcasted_iota(jnp.int32, (DFT_N1, DFT_N1), 0)
    k = lax.broadcasted_iota(jnp.int32, (DFT_N1, DFT_N1), 1)
    ang = ((n * k) % DFT_N1).astype(F32) * (2.0 * math.pi / DFT_N1)
    fc = jnp.cos(ang).astype(BF16)
    fs = jnp.sin(ang).astype(BF16)
    j = lax.broadcasted_iota(jnp.int32, (F_WIDTH, F_WIDTH), 0)
    jp = lax.broadcasted_iota(jnp.int32, (F_WIDTH, F_WIDTH), 1)
    same = (j // F_GW) == (jp // F_GW)
    ang_c = (((j % F_GW) * (jp % F_GW)) % F_GW).astype(F32) * (2.0 * math.pi / F_GW)
    cc = jnp.where(same, jnp.cos(ang_c), 0.0).astype(BF16)
    sc = jnp.where(same, jnp.sin(ang_c), 0.0).astype(BF16)
    return fc, fs, cc, sc


def _trunk(x, batch, seq, p, tables):
    fc, fs, cc, sc = tables
    slopes = 2.0 ** (-8.0 * jnp.arange(1, ATT_HEADS + 1, dtype=F32) / ATT_HEADS)
    for l in range(DEPTH):
        lam_init = 0.8 - 0.6 * math.exp(-0.3 * l)
        x1, rxg, qkv, fx = _ffn_inproj(x, p["ffn1_wg"][l], p["ffn1_wu"][l], p["ffn1_wd"][l],
                                       p["ln_g"][l, 0:1], p["ln_b"][l, 0:1], p["w_in"][l])
        hf, hb = _rglru(rxg, batch, seq, p["conv_w"][l], p["conv_b"][l], p["wgate"][l], p["bgate"][l],
                        p["rg_lambda"][l])
        yb = _attention(qkv, batch, seq, slopes, p["lambda_qk"][l], p["subln_g"][l], lam_init)
        yc = _fourier(fx, batch, seq, fc, fs, cc, sc)
        x = _outproj_ffn(x1, hf, hb, rxg, yb, yc, p["w_out"][l], p["ln_g"][l, 1:2], p["ln_b"][l, 1:2],
                         p["ffn2_wg"][l], p["ffn2_wu"][l], p["ffn2_wd"][l], p["ln_g"][l, 2:3], p["ln_b"][l, 2:3])
    return x


def kernel(x_prompt, x_sample, ln_g, ln_b, ffn1_wg, ffn1_wu, ffn1_wd, ffn2_wg, ffn2_wu, ffn2_wd, w_in, conv_w, conv_b,
           rg_wa, rg_ba, rg_wx, rg_bx, rg_lambda, lambda_qk, subln_g, w_out):
    wgate = jnp.stack([
        jnp.stack([jnp.concatenate([_block_diag(rg_wa[l, d]), _block_diag(rg_wx[l, d])], axis=1)
                   for d in range(2)]) for l in range(DEPTH)]).astype(BF16)
    bgate = jnp.concatenate([rg_ba, rg_bx], axis=-1)[:, :, None, :]
    p = dict(
        ln_g=ln_g, ln_b=ln_b,
        ffn1_wg=ffn1_wg.astype(BF16), ffn1_wu=ffn1_wu.astype(BF16), ffn1_wd=ffn1_wd.astype(BF16),
        ffn2_wg=ffn2_wg.astype(BF16), ffn2_wu=ffn2_wu.astype(BF16), ffn2_wd=ffn2_wd.astype(BF16),
        w_in=w_in.astype(BF16), w_out=w_out.astype(BF16),
        conv_w=conv_w, conv_b=conv_b[:, None, :],
        wgate=wgate, bgate=bgate, rg_lambda=rg_lambda[:, :, None, :],
        lambda_qk=lambda_qk, subln_g=subln_g[:, :, None],
    )
    tables = _dft_tables()
    outs = []
    for x in (x_prompt, x_sample):
        batch, seq, _ = x.shape
        y = _trunk(x.reshape(batch * seq, D_MODEL), batch, seq, p, tables)
        outs.append(y.reshape(batch, seq, D_MODEL))
    return tuple(outs)
```

```python
import functools
import math

import numpy as np
import jax
import jax.numpy as jnp
from jax import lax
from jax.experimental import pallas as pl
from jax.experimental.pallas import tpu as pltpu

D_MODEL = 1024
DEPTH = 2
D_FF = 2816
RG_WIDTH = 256
RG_HEADS = 4
RG_BW = 64
RG_C = 8.0
CONV_W = 4
ATT_HEADS = 4
HEAD_DIM = 64
ATT_WIDTH = 512
F_WIDTH = 256
F_GROUPS = 4
F_GW = 64
IN_WIDTH = 2304
ALPHA = (2.0 * DEPTH) ** 0.25
LN_EPS = 1e-5
NORM_EPS = 1e-5

BF16 = jnp.bfloat16
F32 = jnp.float32

VMEM_LIMIT_BYTES = 56 * 1024 * 1024
SUBLANES = 8

TOKEN_TILE = 1024
SUB_TILE = 512
TOKEN_VMEM_LIMIT_BYTES = 62 * 1024 * 1024
FF_CHUNKS = ((0, 1024), (1024, 2048), (2048, 2816))
SCAN_CHUNK = 1024
ATT_TQ = 256
ATT_TQ_PLAIN = 512
ATT_TK = 512
ATT_PREFETCH_MIN_STEPS = 16
POS_BLOCK = 256
KA_WIDTH = 256
Q_PRESCALE = HEAD_DIM ** -0.5 * math.log2(math.e)
VT_ROWS = 144
DFT_N1 = 1024
DFT_TM = 256


def _cparams(n_axes, vmem_limit_bytes=VMEM_LIMIT_BYTES):
    return pltpu.CompilerParams(
        dimension_semantics=("arbitrary",) * n_axes,
        vmem_limit_bytes=vmem_limit_bytes,
    )


def _dot(a, b):
    return jnp.dot(a, b, preferred_element_type=F32)


def _layer_norm(y, g, b):
    mu = jnp.mean(y, axis=-1, keepdims=True)
    yc = y - mu
    var = jnp.mean(yc * yc, axis=-1, keepdims=True)
    return yc * lax.rsqrt(var + LN_EPS) * g + b


def _swiglu(xb, wg_ref, wu_ref, wd_ref):
    acc = None
    for lo, hi in FF_CHUNKS:
        hg = _dot(xb, wg_ref[:, lo:hi])
        hu = _dot(xb, wu_ref[:, lo:hi])
        h = (hg * jax.nn.sigmoid(hg) * hu).astype(BF16)
        part = _dot(h, wd_ref[lo:hi, :])
        acc = part if acc is None else acc + part
    return acc


def _ffn_inproj_kernel(x_ref, wg_ref, wu_ref, wd_ref, g_ref, b_ref, win_ref,
                       xo_ref, rxg_ref, qkv_ref, fx_ref):
    q_lo, k_lo, v_hi = 2 * RG_WIDTH, 2 * RG_WIDTH + ATT_WIDTH, 2 * RG_WIDTH + 3 * ATT_WIDTH
    subs = [slice(r0, r0 + SUB_TILE) for r0 in range(0, x_ref.shape[0], SUB_TILE)]
    ys = []
    for rows in subs:
        x = x_ref[rows, :]
        ys.append(ALPHA * x + 0.5 * _swiglu(x.astype(BF16), wg_ref, wu_ref, wd_ref))
    for rows, y in zip(subs, ys):
        x1 = _layer_norm(y, g_ref[...], b_ref[...])
        xo_ref[rows, :] = x1
        xb = x1.astype(BF16)
        rxg_ref[rows, :] = _dot(xb, win_ref[:, 0:q_lo])
        qkv_ref[rows, 0:ATT_WIDTH] = (_dot(xb, win_ref[:, q_lo:k_lo]) * Q_PRESCALE).astype(BF16)
        qkv_ref[rows, ATT_WIDTH:3 * ATT_WIDTH] = _dot(xb, win_ref[:, k_lo:v_hi]).astype(BF16)
        fx_ref[rows, :] = _dot(xb, win_ref[:, v_hi:IN_WIDTH]).astype(BF16)


def _full(shape):
    return pl.BlockSpec(shape, lambda *_: (0,) * len(shape))


def _ffn_inproj(x, wg, wu, wd, g, b, win):
    t = x.shape[0]
    tm = TOKEN_TILE
    row = lambda w: pl.BlockSpec((tm, w), lambda i: (i, 0))
    return pl.pallas_call(
        _ffn_inproj_kernel,
        grid=(t // tm,),
        in_specs=[row(D_MODEL), _full(wg.shape), _full(wu.shape), _full(wd.shape),
                  _full(g.shape), _full(b.shape), _full(win.shape)],
        out_specs=[row(D_MODEL), row(2 * RG_WIDTH), row(3 * ATT_WIDTH), row(F_WIDTH)],
        out_shape=[jax.ShapeDtypeStruct((t, D_MODEL), F32),
                   jax.ShapeDtypeStruct((t, 2 * RG_WIDTH), F32),
                   jax.ShapeDtypeStruct((t, 3 * ATT_WIDTH), BF16),
                   jax.ShapeDtypeStruct((t, F_WIDTH), BF16)],
        compiler_params=_cparams(1, TOKEN_VMEM_LIMIT_BYTES),
        name="ffn_inproj",
    )(x, wg, wu, wd, g, b, win)


def _outproj_ffn_kernel(x_ref, hf_ref, hb_ref, rg_ref, yb_ref, yc_ref, wo_ref, g1_ref, b1_ref,
                        wg_ref, wu_ref, wd_ref, g2_ref, b2_ref, o_ref):
    subs = [slice(r0, r0 + SUB_TILE) for r0 in range(0, x_ref.shape[0], SUB_TILE)]
    x2s = []
    for rows in subs:
        ya = (jax.nn.gelu(rg_ref[rows, :], approximate=True) * (hf_ref[rows, :] + hb_ref[rows, :])).astype(BF16)
        mix = (_dot(ya, wo_ref[0:RG_WIDTH, :])
               + _dot(yb_ref[rows, :], wo_ref[RG_WIDTH:RG_WIDTH + ATT_WIDTH, :])
               + _dot(yc_ref[rows, :], wo_ref[RG_WIDTH + ATT_WIDTH:D_MODEL, :]))
        x2s.append(_layer_norm(ALPHA * x_ref[rows, :] + mix, g1_ref[...], b1_ref[...]))
    ys = [ALPHA * x2 + 0.5 * _swiglu(x2.astype(BF16), wg_ref, wu_ref, wd_ref) for x2 in x2s]
    for rows, y in zip(subs, ys):
        o_ref[rows, :] = _layer_norm(y, g2_ref[...], b2_ref[...])


def _outproj_ffn(x, hf, hb, rxg, yb, yc, wo, g1, b1, wg, wu, wd, g2, b2):
    t = x.shape[0]
    tm = TOKEN_TILE
    row = lambda w: pl.BlockSpec((tm, w), lambda i: (i, 0))
    rgate = pl.BlockSpec((tm, RG_WIDTH), lambda i: (i, 1))
    return pl.pallas_call(
        _outproj_ffn_kernel,
        grid=(t // tm,),
        in_specs=[row(D_MODEL), row(RG_WIDTH), row(RG_WIDTH), rgate, row(ATT_WIDTH), row(F_WIDTH),
                  _full(wo.shape), _full(g1.shape), _full(b1.shape),
                  _full(wg.shape), _full(wu.shape), _full(wd.shape), _full(g2.shape), _full(b2.shape)],
        out_specs=row(D_MODEL),
        out_shape=jax.ShapeDtypeStruct((t, D_MODEL), F32),
        compiler_params=_cparams(1, TOKEN_VMEM_LIMIT_BYTES),
        name="outproj_ffn",
    )(x, hf, hb, rxg, yb, yc, wo, g1, b1, wg, wu, wd, g2, b2)


def _rglru_direction(x_ref, prev_ref, next_ref, chunk, n_chunks, d, reverse,
                     cw_ref, cb_ref, wgate_ref, bgate_ref, lam_ref,
                     a_scr, u_scr, carry_scr, h_ref):
    tc = x_ref.shape[0]
    n_groups = tc // SUBLANES
    grp = (n_groups, SUBLANES, RG_WIDTH)
    rig = lax.broadcasted_iota(jnp.int32, (1, SUBLANES, 1), 1)
    x = x_ref[...].reshape(grp)
    prev = (prev_ref[...] * jnp.where(chunk > 0, 1.0, 0.0)).reshape(1, SUBLANES, RG_WIDTH)
    nxt = (next_ref[...] * jnp.where(chunk < n_chunks - 1, 1.0, 0.0)).reshape(1, SUBLANES, RG_WIDTH)

    def delayed(k):
        y = pltpu.roll(x, k, 1)
        before = jnp.concatenate([pltpu.roll(prev, k, 1), y[:-1]], axis=0)
        return jnp.where(rig >= k, y, before)

    ahead = pltpu.roll(x, SUBLANES - 1, 1)
    after = jnp.concatenate([ahead[1:], pltpu.roll(nxt, SUBLANES - 1, 1)], axis=0)
    xp1 = jnp.where(rig < SUBLANES - 1, ahead, after)
    cw = cw_ref[...]
    xc = cb_ref[...] + delayed(2) * cw[0:1, :] + delayed(1) * cw[1:2, :] + x * cw[2:3, :] + xp1 * cw[3:4, :]
    xc = xc.reshape(tc, RG_WIDTH)

    gates = _dot(xc.astype(BF16), wgate_ref[d]) + bgate_ref[d]
    r = jax.nn.sigmoid(gates[:, 0:RG_WIDTH])
    i = jax.nn.sigmoid(gates[:, RG_WIDTH:2 * RG_WIDTH])
    nlam = -lam_ref[d]
    softplus = jnp.maximum(nlam, 0.0) + jnp.log1p(jnp.exp(-jnp.abs(nlam)))
    log_a = (-RG_C * softplus) * r
    a = jnp.exp(log_a)
    th = jnp.tanh(log_a)
    u = jnp.sqrt(-2.0 * th / (1.0 - th)) * (i * xc)

    u = u.reshape(grp)
    a = a.reshape(grp)
    for k in (1, 2, 4):
        if reverse:
            keep = rig < SUBLANES - k
            shift = SUBLANES - k
        else:
            keep = rig >= k
            shift = k
        us = jnp.where(keep, pltpu.roll(u, shift, 1), 0.0)
        as_ = jnp.where(keep, pltpu.roll(a, shift, 1), 1.0)
        u = u + a * us
        a = a * as_
    a_scr[...] = a.reshape(tc, RG_WIDTH)
    u_scr[...] = u.reshape(tc, RG_WIDTH)


    def body(g, h):
        gi = n_groups - 1 - g if reverse else g
        r0 = pl.multiple_of(gi * SUBLANES, SUBLANES)
        hg = u_scr[pl.ds(r0, SUBLANES), :] + a_scr[pl.ds(r0, SUBLANES), :] * h
        h_ref[pl.ds(r0, SUBLANES), :] = hg
        edge = hg[0:1, :] if reverse else hg[SUBLANES - 1:SUBLANES, :]
        return jnp.broadcast_to(edge, (SUBLANES, RG_WIDTH))

    carry_scr[...] = lax.fori_loop(0, n_groups, body, carry_scr[...], unroll=8)


def _rglru_kernel(xf_ref, pf_ref, nf_ref, xb_ref, pb_ref, nb_ref,
                  cw_ref, cb_ref, wgate_ref, bgate_ref, lam_ref,
                  hf_ref, hb_ref, af_scr, uf_scr, ab_scr, ub_scr, cf_scr, cbk_scr, *, n_chunks):
    c = pl.program_id(1)

    @pl.when(c == 0)
    def _():
        cf_scr[...] = jnp.zeros_like(cf_scr)
        cbk_scr[...] = jnp.zeros_like(cbk_scr)

    _rglru_direction(xf_ref, pf_ref, nf_ref, c, n_chunks, 0, False,
                     cw_ref, cb_ref, wgate_ref, bgate_ref, lam_ref, af_scr, uf_scr, cf_scr, hf_ref)
    _rglru_direction(xb_ref, pb_ref, nb_ref, n_chunks - 1 - c, n_chunks, 1, True,
                     cw_ref, cb_ref, wgate_ref, bgate_ref, lam_ref, ab_scr, ub_scr, cbk_scr, hb_ref)


def _rglru(rxg, batch, seq, cw, cb, wgate, bgate, lam):
    t = rxg.shape[0]
    tc = SCAN_CHUNK
    nch = seq // tc
    hb8 = tc // SUBLANES
    last8 = t // SUBLANES - 1

    def main_f(b, c):
        return (b * nch + c, 0)

    def prev_f(b, c):
        return (jnp.maximum((b * nch + c) * hb8 - 1, 0), 0)

    def next_f(b, c):
        return (jnp.minimum((b * nch + c + 1) * hb8, last8), 0)

    def main_b(b, c):
        return (b * nch + (nch - 1 - c), 0)

    def prev_b(b, c):
        return (jnp.maximum((b * nch + (nch - 1 - c)) * hb8 - 1, 0), 0)

    def next_b(b, c):
        return (jnp.minimum((b * nch + (nch - 1 - c) + 1) * hb8, last8), 0)

    blk = lambda f: pl.BlockSpec((tc, RG_WIDTH), f)
    halo = lambda f: pl.BlockSpec((SUBLANES, RG_WIDTH), f)
    return pl.pallas_call(
        functools.partial(_rglru_kernel, n_chunks=nch),
        grid=(batch, nch),
        in_specs=[blk(main_f), halo(prev_f), halo(next_f), blk(main_b), halo(prev_b), halo(next_b),
                  _full(cw.shape), _full(cb.shape), _full(wgate.shape), _full(bgate.shape), _full(lam.shape)],
        out_specs=[blk(main_f), blk(main_b)],
        out_shape=[jax.ShapeDtypeStruct((t, RG_WIDTH), F32), jax.ShapeDtypeStruct((t, RG_WIDTH), F32)],
        scratch_shapes=[pltpu.VMEM((tc, RG_WIDTH), F32)] * 4 + [pltpu.VMEM((SUBLANES, RG_WIDTH), F32)] * 2,
        compiler_params=_cparams(2),
        name="rglru",
    )(rxg, rxg, rxg, rxg, rxg, rxg, cw, cb, wgate, bgate, lam)


def _bf16_split3(x):
    out = []
    for _ in range(3):
        hi = float(np.asarray(x, np.float32).astype(jnp.bfloat16).astype(np.float32))
        out.append(hi)
        x = x - hi
    return out


LOG2E_PARTS = _bf16_split3(math.log2(math.e))
EXT_COLS = 12


def _attn_kernel(slopes_ref, q_ref, k_ref, v_ref, lq_ref, g_ref, o_ref, ka_scr, vt_scr, qa_scr, s_scr, p_scr, acc_scr,
                 *, seq, tq, tk, lam_init):
    h = pl.program_id(1)
    nk = seq // tk
    nq = seq // tq
    prefetch_head = nk >= ATT_PREFETCH_MIN_STEPS
    slope = slopes_ref[h]
    d2 = 2 * HEAD_DIM
    l1, l2, l3 = LOG2E_PARTS

    def log2e_piece(idx):
        r = idx % 3
        return jnp.where(r == 0, l1, jnp.where(r == 1, l2, l3))

    ka_scr[:, 0:d2] = k_ref[...]
    pos = lax.broadcasted_iota(jnp.int32, (seq, d2), 0)
    col = lax.broadcasted_iota(jnp.int32, (seq, d2), 1)
    rem = pos & (POS_BLOCK - 1)
    ext = jnp.where(col < 6, log2e_piece(col),
                    jnp.where(col < 9, rem.astype(F32) * slope,
                              jnp.where(col < EXT_COLS, (pos - rem).astype(F32) * slope, 0.0)))
    ka_scr[:, d2:2 * d2] = ext.astype(BF16)
    ones_rows = jnp.where(lax.broadcasted_iota(jnp.int32, (VT_ROWS - d2, tk), 0) == 0, 1.0, 0.0).astype(BF16)
    for j in range(nk):
        vt_scr[j, 0:d2, :] = v_ref[j * tk:(j + 1) * tk, :].astype(F32).T.astype(BF16)
        vt_scr[j, d2:VT_ROWS, :] = ones_rows

    lq = lq_ref[...]
    lam = (jnp.exp(jnp.sum(lq[0:1, :] * lq[1:2, :], axis=1, keepdims=True))
           - jnp.exp(jnp.sum(lq[2:3, :] * lq[3:4, :], axis=1, keepdims=True)) + lam_init)
    rowi = lax.broadcasted_iota(jnp.int32, (d2, tq), 0)
    coli = lax.broadcasted_iota(jnp.int32, (d2, tq), 1)

    def tile_of(i, diag):
        t = i - 1
        return jnp.where(i == 0, diag, t + jnp.where(t >= diag, 1, 0))

    def keys(tile):
        return ka_scr[pl.ds(pl.multiple_of(tile * tk, tk), tk), :]

    def scores_into(slot, i, diag):
        tile = tile_of(i, diag)
        ka = keys(tile)
        var = jnp.where(tile > diag, 1, 0)
        for c in range(2):
            s_scr[slot, c] = _dot(ka, qa_scr[2 * var + c])

    def head(qi):
        q0 = pl.multiple_of(qi * tq, tq)
        qt = q_ref[pl.ds(q0, tq), :].astype(F32).T
        posq = q0 + coli
        remq = posq & (POS_BLOCK - 1)
        ext_t = jnp.where(rowi < 3, -(remq.astype(F32) * slope),
                          jnp.where(rowi < 6, -((posq - remq).astype(F32) * slope),
                                    jnp.where(rowi < EXT_COLS, log2e_piece(rowi), 0.0)))
        for c in range(2):
            qc_t = jnp.where((rowi >= c * HEAD_DIM) & (rowi < (c + 1) * HEAD_DIM), qt, 0.0).astype(BF16)
            for var in range(2):
                qa_scr[2 * var + c, 0:d2, :] = qc_t
                qa_scr[2 * var + c, d2:2 * d2, :] = (ext_t if var == 0 else -ext_t).astype(BF16)
        diag = q0 // tk
        kd = keys(diag)
        for c in range(2):
            s_scr[0, c] = jnp.minimum(_dot(kd, qa_scr[c]), _dot(kd, qa_scr[2 + c]))
        scores_into(1, 1, diag)

    def q_tile(qi):
        q0 = pl.multiple_of(qi * tq, tq)
        diag = q0 // tk
        if not prefetch_head:
            head(qi)

        def softmax_step(slot, c, m):
            s = s_scr[slot, c]
            mn = jnp.max(s, axis=0, keepdims=True)
            al = None
            if m is not None:
                mn = jnp.maximum(m, mn)
                al = jnp.exp2(m - mn)
            p_scr[slot, c] = jnp.exp2(s - mn).astype(BF16)
            return mn, al

        def finish(slot, i, al):
            vt = vt_scr[tile_of(i, diag)]
            for c in range(2):
                acc_scr[c] = al[c] * acc_scr[c] + _dot(vt, p_scr[slot, c])

        def step(i, cur, m0, m1, al, last=False):
            nxt = 1 - cur
            finish(nxt, i - 1, al)
            m0, al0 = softmax_step(cur, 0, m0)
            m1, al1 = softmax_step(cur, 1, m1)
            if not last:
                scores_into(nxt, i + 1, diag)
            return m0, m1, (al0, al1)

        m0, _ = softmax_step(0, 0, None)
        m1, _ = softmax_step(0, 1, None)
        acc_scr[...] = jnp.zeros_like(acc_scr)
        al = (jnp.ones((1, tq), F32),) * 2
        for i in range(1, nk):
            m0, m1, al = step(i, i % 2, m0, m1, al, last=(i == nk - 1))
        finish(1, nk - 1, al)
        if prefetch_head:
            head(jnp.minimum(qi + 1, nq - 1))
        a0 = acc_scr[0]
        a1 = acc_scr[1]
        o = a0[0:d2, :] / a0[d2:d2 + 1, :] - lam * (a1[0:d2, :] / a1[d2:d2 + 1, :])
        o = o * lax.rsqrt(jnp.mean(o * o, axis=0, keepdims=True) + NORM_EPS) * g_ref[...] * (1.0 - lam_init)
        o_ref[pl.ds(q0, tq), :] = o.T.astype(BF16)

    if prefetch_head:
        head(0)

    def q_loop(qi, carry):
        q_tile(qi)
        return carry

    lax.fori_loop(0, nq, q_loop, 0)


def _attention(qkv, batch, seq, slopes, lq, g_col, lam_init):
    t = qkv.shape[0]
    tk = ATT_TK
    tq = ATT_TQ if seq // tk >= ATT_PREFETCH_MIN_STEPS else ATT_TQ_PLAIN
    kernel = functools.partial(_attn_kernel, seq=seq, tq=tq, tk=tk, lam_init=lam_init)
    col = lambda off: pl.BlockSpec((seq, 2 * HEAD_DIM), lambda b, h: (b, off + h))
    return pl.pallas_call(
        kernel,
        grid=(batch, ATT_HEADS),
        in_specs=[pl.BlockSpec(memory_space=pltpu.SMEM), col(0), col(ATT_HEADS), col(2 * ATT_HEADS),
                  _full(lq.shape), _full(g_col.shape)],
        out_specs=col(0),
        out_shape=jax.ShapeDtypeStruct((t, ATT_WIDTH), BF16),
        scratch_shapes=[pltpu.VMEM((seq, KA_WIDTH), BF16),
                        pltpu.VMEM((seq // tk, VT_ROWS, tk), BF16),
                        pltpu.VMEM((4, KA_WIDTH, tq), BF16),
                        pltpu.VMEM((2, 2, tk, tq), F32),
                        pltpu.VMEM((2, 2, tk, tq), BF16),
                        pltpu.VMEM((2, VT_ROWS, tq), F32)],
        compiler_params=_cparams(2),
        name="diff_attention",
    )(slopes, qkv, qkv, qkv, lq, g_col)


def _snap(v):
    r = round(v)
    return float(r) if abs(v - r) < 1e-12 else v


def _fourier_kernel(x_ref, fc_ref, fs_ref, twc_ref, tws_ref, cc_ref, sc_ref, o_ref, *, n2, scale):
    x = x_ref[...]
    ar = _dot(fc_ref[...], x)
    ai = _dot(fs_ref[...], x)
    c = twc_ref[...]
    s = tws_ref[...]
    br = ar * c - ai * s
    bi = -(ar * s + ai * c)
    cc = cc_ref[...]
    sc = sc_ref[...]
    for k2 in range(n2):
        xr = None
        xi = None
        for m in range(n2):
            ph = (m * k2) % n2
            cp = _snap(math.cos(2.0 * math.pi * ph / n2))
            sp = _snap(math.sin(2.0 * math.pi * ph / n2))
            brn = br[:, m * F_WIDTH:(m + 1) * F_WIDTH]
            bin_ = bi[:, m * F_WIDTH:(m + 1) * F_WIDTH]
            tr = brn * cp + bin_ * sp
            ti = bin_ * cp - brn * sp
            xr = tr if xr is None else xr + tr
            xi = ti if xi is None else xi + ti
        out = (_dot(xr.astype(BF16), cc) + _dot(xi.astype(BF16), sc)) * scale
        o_ref[k2] = out.astype(BF16)


def _fourier(fx, batch, seq, fc, fs, cc, sc):
    n1 = DFT_N1
    n2 = seq // n1
    tm = DFT_TM
    x2 = fx.reshape(batch, n1, n2 * F_WIDTH)
    k1 = lax.broadcasted_iota(jnp.int32, (n1, n2 * F_WIDTH), 0)
    m = lax.broadcasted_iota(jnp.int32, (n1, n2 * F_WIDTH), 1) // F_WIDTH
    ang = ((k1 * m) % seq).astype(F32) * (2.0 * math.pi / seq)
    twc = jnp.cos(ang)
    tws = jnp.sin(ang)
    kernel = functools.partial(_fourier_kernel, n2=n2, scale=1.0 / math.sqrt(seq * F_GW))
    out = pl.pallas_call(
        kernel,
        grid=(n1 // tm, batch),
        in_specs=[pl.BlockSpec((None, n1, n2 * F_WIDTH), lambda i, b: (b, 0, 0)),
                  pl.BlockSpec((tm, n1), lambda i, b: (i, 0)),
                  pl.BlockSpec((tm, n1), lambda i, b: (i, 0)),
                  pl.BlockSpec((tm, n2 * F_WIDTH), lambda i, b: (i, 0)),
                  pl.BlockSpec((tm, n2 * F_WIDTH), lambda i, b: (i, 0)),
                  _full(cc.shape), _full(sc.shape)],
        out_specs=pl.BlockSpec((None, n2, tm, F_WIDTH), lambda i, b: (b, 0, i, 0)),
        out_shape=jax.ShapeDtypeStruct((batch, n2, n1, F_WIDTH), BF16),
        compiler_params=_cparams(2),
        name="fourier",
    )(x2, fc, fs, twc, tws, cc, sc)
    return out.reshape(batch * seq, F_WIDTH)


def _block_diag(w):
    eye = jnp.eye(RG_HEADS, dtype=w.dtype)
    return (eye[:, None, :, None] * w[:, :, None, :]).reshape(RG_WIDTH, RG_WIDTH)


def _dft_tables():
    n = lax.broadcasted_iota(jnp.int32, (DFT_N1, DFT_N1), 0)
    k = lax.broadcasted_iota(jnp.int32, (DFT_N1, DFT_N1), 1)
    ang = ((n * k) % DFT_N1).astype(F32) * (2.0 * math.pi / DFT_N1)
    fc = jnp.cos(ang).astype(BF16)
    fs = jnp.sin(ang).astype(BF16)
    j = lax.broadcasted_iota(jnp.int32, (F_WIDTH, F_WIDTH), 0)
    jp = lax.broadcasted_iota(jnp.int32, (F_WIDTH, F_WIDTH), 1)
    same = (j // F_GW) == (jp // F_GW)
    ang_c = (((j % F_GW) * (jp % F_GW)) % F_GW).astype(F32) * (2.0 * math.pi / F_GW)
    cc = jnp.where(same, jnp.cos(ang_c), 0.0).astype(BF16)
    sc = jnp.where(same, jnp.sin(ang_c), 0.0).astype(BF16)
    return fc, fs, cc, sc


def _trunk(x, batch, seq, p, tables):
    fc, fs, cc, sc = tables
    slopes = 2.0 ** (-8.0 * jnp.arange(1, ATT_HEADS + 1, dtype=F32) / ATT_HEADS)
    for l in range(DEPTH):
        lam_init = 0.8 - 0.6 * math.exp(-0.3 * l)
        x1, rxg, qkv, fx = _ffn_inproj(x, p["ffn1_wg"][l], p["ffn1_wu"][l], p["ffn1_wd"][l],
                                       p["ln_g"][l, 0:1], p["ln_b"][l, 0:1], p["w_in"][l])
        hf, hb = _rglru(rxg, batch, seq, p["conv_w"][l], p["conv_b"][l], p["wgate"][l], p["bgate"][l],
                        p["rg_lambda"][l])
        yb = _attention(qkv, batch, seq, slopes, p["lambda_qk"][l], p["subln_g"][l], lam_init)
        yc = _fourier(fx, batch, seq, fc, fs, cc, sc)
        x = _outproj_ffn(x1, hf, hb, rxg, yb, yc, p["w_out"][l], p["ln_g"][l, 1:2], p["ln_b"][l, 1:2],
                         p["ffn2_wg"][l], p["ffn2_wu"][l], p["ffn2_wd"][l], p["ln_g"][l, 2:3], p["ln_b"][l, 2:3])
    return x


def kernel(x_prompt, x_sample, ln_g, ln_b, ffn1_wg, ffn1_wu, ffn1_wd, ffn2_wg, ffn2_wu, ffn2_wd, w_in, conv_w, conv_b,
           rg_wa, rg_ba, rg_wx, rg_bx, rg_lambda, lambda_qk, subln_g, w_out):
    wgate = jnp.stack([
        jnp.stack([jnp.concatenate([_block_diag(rg_wa[l, d]), _block_diag(rg_wx[l, d])], axis=1)
                   for d in range(2)]) for l in range(DEPTH)]).astype(BF16)
    bgate = jnp.concatenate([rg_ba, rg_bx], axis=-1)[:, :, None, :]
    p = dict(
        ln_g=ln_g, ln_b=ln_b,
        ffn1_wg=ffn1_wg.astype(BF16), ffn1_wu=ffn1_wu.astype(BF16), ffn1_wd=ffn1_wd.astype(BF16),
        ffn2_wg=ffn2_wg.astype(BF16), ffn2_wu=ffn2_wu.astype(BF16), ffn2_wd=ffn2_wd.astype(BF16),
        w_in=w_in.astype(BF16), w_out=w_out.astype(BF16),
        conv_w=conv_w, conv_b=conv_b[:, None, :],
        wgate=wgate, bgate=bgate, rg_lambda=rg_lambda[:, :, None, :],
        lambda_qk=lambda_qk, subln_g=subln_g[:, :, None],
    )
    tables = _dft_tables()
    outs = []
    for x in (x_prompt, x_sample):
        batch, seq, _ = x.shape
        y = _trunk(x.reshape(batch * seq, D_MODEL), batch, seq, p, tables)
        outs.append(y.reshape(batch, seq, D_MODEL))
    return tuple(outs)
```

```python
import functools
import itertools
import math

import numpy as np
import jax
import jax.numpy as jnp
from jax import lax
from jax.experimental import pallas as pl
from jax.experimental.pallas import tpu as pltpu

D_MODEL = 1024
DEPTH = 2
D_FF = 2816
RG_WIDTH = 256
RG_HEADS = 4
RG_BW = 64
RG_C = 8.0
CONV_W = 4
ATT_HEADS = 4
HEAD_DIM = 64
ATT_WIDTH = 512
F_WIDTH = 256
F_GROUPS = 4
F_GW = 64
IN_WIDTH = 2304
ALPHA = (2.0 * DEPTH) ** 0.25
LN_EPS = 1e-5
NORM_EPS = 1e-5

BF16 = jnp.bfloat16
F32 = jnp.float32

VMEM_LIMIT_BYTES = 56 * 1024 * 1024
SUBLANES = 8

TOKEN_TILE = 1024
SUB_TILE = 512
TOKEN_VMEM_LIMIT_BYTES = 62 * 1024 * 1024
FF_CHUNKS = ((0, 1024), (1024, 2048), (2048, 2816))
SCAN_CHUNK = 1024
ATT_TQ = 256
ATT_TQ_PLAIN = 512
ATT_TK = 512
ATT_PREFETCH_MIN_STEPS = 16
POS_BLOCK = 256
KA_WIDTH = 256
Q_PRESCALE = HEAD_DIM ** -0.5 * math.log2(math.e)
VT_ROWS = 144
DFT_N1 = 1024
DFT_TM = 256


def _cparams(n_axes, vmem_limit_bytes=VMEM_LIMIT_BYTES):
    return pltpu.CompilerParams(
        dimension_semantics=("arbitrary",) * n_axes,
        vmem_limit_bytes=vmem_limit_bytes,
    )


def _dot(a, b):
    return jnp.dot(a, b, preferred_element_type=F32)


def _layer_norm(y, g, b):
    mu = jnp.mean(y, axis=-1, keepdims=True)
    yc = y - mu
    var = jnp.mean(yc * yc, axis=-1, keepdims=True)
    return yc * lax.rsqrt(var + LN_EPS) * g + b


def _swiglu(xb, wg_ref, wu_ref, wd_ref):
    acc = None
    for lo, hi in FF_CHUNKS:
        hg = _dot(xb, wg_ref[:, lo:hi])
        hu = _dot(xb, wu_ref[:, lo:hi])
        h = (hg * jax.nn.sigmoid(hg) * hu).astype(BF16)
        part = _dot(h, wd_ref[lo:hi, :])
        acc = part if acc is None else acc + part
    return acc


def _ffn_inproj_kernel(x_ref, wg_ref, wu_ref, wd_ref, g_ref, b_ref, win_ref,
                       xo_ref, rxg_ref, qkv_ref, fx_ref):
    q_lo, k_lo, v_hi = 2 * RG_WIDTH, 2 * RG_WIDTH + ATT_WIDTH, 2 * RG_WIDTH + 3 * ATT_WIDTH
    subs = [slice(r0, r0 + SUB_TILE) for r0 in range(0, x_ref.shape[0], SUB_TILE)]
    ys = []
    for rows in subs:
        x = x_ref[rows, :]
        ys.append(ALPHA * x + 0.5 * _swiglu(x.astype(BF16), wg_ref, wu_ref, wd_ref))
    for rows, y in zip(subs, ys):
        x1 = _layer_norm(y, g_ref[...], b_ref[...])
        xo_ref[rows, :] = x1
        xb = x1.astype(BF16)
        rxg_ref[rows, :] = _dot(xb, win_ref[:, 0:q_lo])
        qkv_ref[rows, 0:ATT_WIDTH] = (_dot(xb, win_ref[:, q_lo:k_lo]) * Q_PRESCALE).astype(BF16)
        qkv_ref[rows, ATT_WIDTH:3 * ATT_WIDTH] = _dot(xb, win_ref[:, k_lo:v_hi]).astype(BF16)
        fx_ref[rows, :] = _dot(xb, win_ref[:, v_hi:IN_WIDTH]).astype(BF16)


def _full(shape):
    return pl.BlockSpec(shape, lambda *_: (0,) * len(shape))


def _ffn_inproj(x, wg, wu, wd, g, b, win):
    t = x.shape[0]
    tm = TOKEN_TILE
    row = lambda w: pl.BlockSpec((tm, w), lambda i: (i, 0))
    return pl.pallas_call(
        _ffn_inproj_kernel,
        grid=(t // tm,),
        in_specs=[row(D_MODEL), _full(wg.shape), _full(wu.shape), _full(wd.shape),
                  _full(g.shape), _full(b.shape), _full(win.shape)],
        out_specs=[row(D_MODEL), row(2 * RG_WIDTH), row(3 * ATT_WIDTH), row(F_WIDTH)],
        out_shape=[jax.ShapeDtypeStruct((t, D_MODEL), F32),
                   jax.ShapeDtypeStruct((t, 2 * RG_WIDTH), F32),
                   jax.ShapeDtypeStruct((t, 3 * ATT_WIDTH), BF16),
                   jax.ShapeDtypeStruct((t, F_WIDTH), BF16)],
        compiler_params=_cparams(1, TOKEN_VMEM_LIMIT_BYTES),
        name="ffn_inproj",
    )(x, wg, wu, wd, g, b, win)


def _outproj_ffn_kernel(x_ref, hf_ref, hb_ref, rg_ref, yb_ref, yc_ref, wo_ref, g1_ref, b1_ref,
                        wg_ref, wu_ref, wd_ref, g2_ref, b2_ref, o_ref):
    subs = [slice(r0, r0 + SUB_TILE) for r0 in range(0, x_ref.shape[0], SUB_TILE)]
    x2s = []
    for rows in subs:
        ya = (jax.nn.gelu(rg_ref[rows, :], approximate=True) * (hf_ref[rows, :] + hb_ref[rows, :])).astype(BF16)
        mix = (_dot(ya, wo_ref[0:RG_WIDTH, :])
               + _dot(yb_ref[rows, :], wo_ref[RG_WIDTH:RG_WIDTH + ATT_WIDTH, :])
               + _dot(yc_ref[rows, :], wo_ref[RG_WIDTH + ATT_WIDTH:D_MODEL, :]))
        x2s.append(_layer_norm(ALPHA * x_ref[rows, :] + mix, g1_ref[...], b1_ref[...]))
    ys = [ALPHA * x2 + 0.5 * _swiglu(x2.astype(BF16), wg_ref, wu_ref, wd_ref) for x2 in x2s]
    for rows, y in zip(subs, ys):
        o_ref[rows, :] = _layer_norm(y, g2_ref[...], b2_ref[...])


def _outproj_ffn(x, hf, hb, rxg, yb, yc, wo, g1, b1, wg, wu, wd, g2, b2):
    t = x.shape[0]
    tm = TOKEN_TILE
    row = lambda w: pl.BlockSpec((tm, w), lambda i: (i, 0))
    rgate = pl.BlockSpec((tm, RG_WIDTH), lambda i: (i, 1))
    return pl.pallas_call(
        _outproj_ffn_kernel,
        grid=(t // tm,),
        in_specs=[row(D_MODEL), row(RG_WIDTH), row(RG_WIDTH), rgate, row(ATT_WIDTH), row(F_WIDTH),
                  _full(wo.shape), _full(g1.shape), _full(b1.shape),
                  _full(wg.shape), _full(wu.shape), _full(wd.shape), _full(g2.shape), _full(b2.shape)],
        out_specs=row(D_MODEL),
        out_shape=jax.ShapeDtypeStruct((t, D_MODEL), F32),
        compiler_params=_cparams(1, TOKEN_VMEM_LIMIT_BYTES),
        name="outproj_ffn",
    )(x, hf, hb, rxg, yb, yc, wo, g1, b1, wg, wu, wd, g2, b2)


def _rglru_direction(x_ref, prev_ref, next_ref, chunk, n_chunks, d, reverse,
                     cw_ref, cb_ref, wgate_ref, bgate_ref, lam_ref,
                     a_scr, u_scr, carry_scr, h_ref):
    tc = x_ref.shape[0]
    n_groups = tc // SUBLANES
    grp = (n_groups, SUBLANES, RG_WIDTH)
    rig = lax.broadcasted_iota(jnp.int32, (1, SUBLANES, 1), 1)
    x = x_ref[...].reshape(grp)
    prev = (prev_ref[...] * jnp.where(chunk > 0, 1.0, 0.0)).reshape(1, SUBLANES, RG_WIDTH)
    nxt = (next_ref[...] * jnp.where(chunk < n_chunks - 1, 1.0, 0.0)).reshape(1, SUBLANES, RG_WIDTH)

    def delayed(k):
        y = pltpu.roll(x, k, 1)
        before = jnp.concatenate([pltpu.roll(prev, k, 1), y[:-1]], axis=0)
        return jnp.where(rig >= k, y, before)

    ahead = pltpu.roll(x, SUBLANES - 1, 1)
    after = jnp.concatenate([ahead[1:], pltpu.roll(nxt, SUBLANES - 1, 1)], axis=0)
    xp1 = jnp.where(rig < SUBLANES - 1, ahead, after)
    cw = cw_ref[...]
    xc = cb_ref[...] + delayed(2) * cw[0:1, :] + delayed(1) * cw[1:2, :] + x * cw[2:3, :] + xp1 * cw[3:4, :]
    xc = xc.reshape(tc, RG_WIDTH)

    gates = _dot(xc.astype(BF16), wgate_ref[d]) + bgate_ref[d]
    r = jax.nn.sigmoid(gates[:, 0:RG_WIDTH])
    i = jax.nn.sigmoid(gates[:, RG_WIDTH:2 * RG_WIDTH])
    nlam = -lam_ref[d]
    softplus = jnp.maximum(nlam, 0.0) + jnp.log1p(jnp.exp(-jnp.abs(nlam)))
    log_a = (-RG_C * softplus) * r
    a = jnp.exp(log_a)
    th = jnp.tanh(log_a)
    u = jnp.sqrt(-2.0 * th / (1.0 - th)) * (i * xc)

    u = u.reshape(grp)
    a = a.reshape(grp)
    for k in (1, 2, 4):
        if reverse:
            keep = rig < SUBLANES - k
            shift = SUBLANES - k
        else:
            keep = rig >= k
            shift = k
        us = jnp.where(keep, pltpu.roll(u, shift, 1), 0.0)
        as_ = jnp.where(keep, pltpu.roll(a, shift, 1), 1.0)
        u = u + a * us
        a = a * as_
    a_scr[...] = a.reshape(tc, RG_WIDTH)
    u_scr[...] = u.reshape(tc, RG_WIDTH)


    def body(g, h):
        gi = n_groups - 1 - g if reverse else g
        r0 = pl.multiple_of(gi * SUBLANES, SUBLANES)
        hg = u_scr[pl.ds(r0, SUBLANES), :] + a_scr[pl.ds(r0, SUBLANES), :] * h
        h_ref[pl.ds(r0, SUBLANES), :] = hg
        edge = hg[0:1, :] if reverse else hg[SUBLANES - 1:SUBLANES, :]
        return jnp.broadcast_to(edge, (SUBLANES, RG_WIDTH))

    carry_scr[...] = lax.fori_loop(0, n_groups, body, carry_scr[...], unroll=8)


def _rglru_kernel(xf_ref, pf_ref, nf_ref, xb_ref, pb_ref, nb_ref,
                  cw_ref, cb_ref, wgate_ref, bgate_ref, lam_ref,
                  hf_ref, hb_ref, af_scr, uf_scr, ab_scr, ub_scr, cf_scr, cbk_scr, *, n_chunks):
    c = pl.program_id(1)

    @pl.when(c == 0)
    def _():
        cf_scr[...] = jnp.zeros_like(cf_scr)
        cbk_scr[...] = jnp.zeros_like(cbk_scr)

    _rglru_direction(xf_ref, pf_ref, nf_ref, c, n_chunks, 0, False,
                     cw_ref, cb_ref, wgate_ref, bgate_ref, lam_ref, af_scr, uf_scr, cf_scr, hf_ref)
    _rglru_direction(xb_ref, pb_ref, nb_ref, n_chunks - 1 - c, n_chunks, 1, True,
                     cw_ref, cb_ref, wgate_ref, bgate_ref, lam_ref, ab_scr, ub_scr, cbk_scr, hb_ref)


def _rglru(rxg, batch, seq, cw, cb, wgate, bgate, lam):
    t = rxg.shape[0]
    tc = SCAN_CHUNK
    nch = seq // tc
    hb8 = tc // SUBLANES
    last8 = t // SUBLANES - 1

    def main_f(b, c):
        return (b * nch + c, 0)

    def prev_f(b, c):
        return (jnp.maximum((b * nch + c) * hb8 - 1, 0), 0)

    def next_f(b, c):
        return (jnp.minimum((b * nch + c + 1) * hb8, last8), 0)

    def main_b(b, c):
        return (b * nch + (nch - 1 - c), 0)

    def prev_b(b, c):
        return (jnp.maximum((b * nch + (nch - 1 - c)) * hb8 - 1, 0), 0)

    def next_b(b, c):
        return (jnp.minimum((b * nch + (nch - 1 - c) + 1) * hb8, last8), 0)

    blk = lambda f: pl.BlockSpec((tc, RG_WIDTH), f)
    halo = lambda f: pl.BlockSpec((SUBLANES, RG_WIDTH), f)
    return pl.pallas_call(
        functools.partial(_rglru_kernel, n_chunks=nch),
        grid=(batch, nch),
        in_specs=[blk(main_f), halo(prev_f), halo(next_f), blk(main_b), halo(prev_b), halo(next_b),
                  _full(cw.shape), _full(cb.shape), _full(wgate.shape), _full(bgate.shape), _full(lam.shape)],
        out_specs=[blk(main_f), blk(main_b)],
        out_shape=[jax.ShapeDtypeStruct((t, RG_WIDTH), F32), jax.ShapeDtypeStruct((t, RG_WIDTH), F32)],
        scratch_shapes=[pltpu.VMEM((tc, RG_WIDTH), F32)] * 4 + [pltpu.VMEM((SUBLANES, RG_WIDTH), F32)] * 2,
        compiler_params=_cparams(2),
        name="rglru",
    )(rxg, rxg, rxg, rxg, rxg, rxg, cw, cb, wgate, bgate, lam)


def _bf16_split3(x):
    out = []
    for _ in range(3):
        hi = float(np.asarray(x, np.float32).astype(jnp.bfloat16).astype(np.float32))
        out.append(hi)
        x = x - hi
    return out


LOG2E_PARTS = _bf16_split3(math.log2(math.e))
EXT_COLS = 12


def _attn_kernel(slopes_ref, q_ref, k_ref, v_ref, lq_ref, g_ref, o_ref, ka_scr, vt_scr, qa_scr, s_scr, p_scr, acc_scr,
                 *, seq, tq, tk, lam_init):
    h = pl.program_id(1)
    nk = seq // tk
    nq = seq // tq
    prefetch_head = nk >= ATT_PREFETCH_MIN_STEPS
    slope = slopes_ref[h]
    d2 = 2 * HEAD_DIM
    l1, l2, l3 = LOG2E_PARTS

    def log2e_piece(idx):
        r = idx % 3
        return jnp.where(r == 0, l1, jnp.where(r == 1, l2, l3))

    ka_scr[:, 0:d2] = k_ref[...]
    pos = lax.broadcasted_iota(jnp.int32, (seq, d2), 0)
    col = lax.broadcasted_iota(jnp.int32, (seq, d2), 1)
    rem = pos & (POS_BLOCK - 1)
    ext = jnp.where(col < 6, log2e_piece(col),
                    jnp.where(col < 9, rem.astype(F32) * slope,
                              jnp.where(col < EXT_COLS, (pos - rem).astype(F32) * slope, 0.0)))
    ka_scr[:, d2:2 * d2] = ext.astype(BF16)
    ones_rows = jnp.where(lax.broadcasted_iota(jnp.int32, (VT_ROWS - d2, tk), 0) == 0, 1.0, 0.0).astype(BF16)
    for j in range(nk):
        vt_scr[j, 0:d2, :] = v_ref[j * tk:(j + 1) * tk, :].astype(F32).T.astype(BF16)
        vt_scr[j, d2:VT_ROWS, :] = ones_rows

    lq = lq_ref[...]
    lam = (jnp.exp(jnp.sum(lq[0:1, :] * lq[1:2, :], axis=1, keepdims=True))
           - jnp.exp(jnp.sum(lq[2:3, :] * lq[3:4, :], axis=1, keepdims=True)) + lam_init)
    rowi = lax.broadcasted_iota(jnp.int32, (d2, tq), 0)
    coli = lax.broadcasted_iota(jnp.int32, (d2, tq), 1)

    def tile_of(i, diag):
        t = i - 1
        return jnp.where(i == 0, diag, t + jnp.where(t >= diag, 1, 0))

    def keys(tile):
        return ka_scr[pl.ds(pl.multiple_of(tile * tk, tk), tk), :]

    def scores_into(slot, i, diag):
        tile = tile_of(i, diag)
        ka = keys(tile)
        var = jnp.where(tile > diag, 1, 0)
        for c in range(2):
            s_scr[slot, c] = _dot(ka, qa_scr[2 * var + c])

    def head(qi):
        q0 = pl.multiple_of(qi * tq, tq)
        qt = q_ref[pl.ds(q0, tq), :].astype(F32).T
        posq = q0 + coli
        remq = posq & (POS_BLOCK - 1)
        ext_t = jnp.where(rowi < 3, -(remq.astype(F32) * slope),
                          jnp.where(rowi < 6, -((posq - remq).astype(F32) * slope),
                                    jnp.where(rowi < EXT_COLS, log2e_piece(rowi), 0.0)))
        for c in range(2):
            qc_t = jnp.where((rowi >= c * HEAD_DIM) & (rowi < (c + 1) * HEAD_DIM), qt, 0.0).astype(BF16)
            for var in range(2):
                qa_scr[2 * var + c, 0:d2, :] = qc_t
                qa_scr[2 * var + c, d2:2 * d2, :] = (ext_t if var == 0 else -ext_t).astype(BF16)
        diag = q0 // tk
        kd = keys(diag)
        for c in range(2):
            s_scr[0, c] = jnp.minimum(_dot(kd, qa_scr[c]), _dot(kd, qa_scr[2 + c]))
        scores_into(1, 1, diag)

    def q_tile(qi):
        q0 = pl.multiple_of(qi * tq, tq)
        diag = q0 // tk
        if not prefetch_head:
            head(qi)

        def softmax_step(slot, c, m):
            s = s_scr[slot, c]
            mn = jnp.max(s, axis=0, keepdims=True)
            al = None
            if m is not None:
                mn = jnp.maximum(m, mn)
                al = jnp.exp2(m - mn)
            p_scr[slot, c] = jnp.exp2(s - mn).astype(BF16)
            return mn, al

        def finish(slot, i, al):
            vt = vt_scr[tile_of(i, diag)]
            for c in range(2):
                acc_scr[c] = al[c] * acc_scr[c] + _dot(vt, p_scr[slot, c])

        def step(i, cur, m0, m1, al, last=False):
            nxt = 1 - cur
            finish(nxt, i - 1, al)
            m0, al0 = softmax_step(cur, 0, m0)
            m1, al1 = softmax_step(cur, 1, m1)
            if not last:
                scores_into(nxt, i + 1, diag)
            return m0, m1, (al0, al1)

        m0, _ = softmax_step(0, 0, None)
        m1, _ = softmax_step(0, 1, None)
        acc_scr[...] = jnp.zeros_like(acc_scr)
        al = (jnp.ones((1, tq), F32),) * 2
        for i in range(1, nk):
            m0, m1, al = step(i, i % 2, m0, m1, al, last=(i == nk - 1))
        finish(1, nk - 1, al)
        if prefetch_head:
            head(jnp.minimum(qi + 1, nq - 1))
        a0 = acc_scr[0]
        a1 = acc_scr[1]
        o = a0[0:d2, :] / a0[d2:d2 + 1, :] - lam * (a1[0:d2, :] / a1[d2:d2 + 1, :])
        o = o * lax.rsqrt(jnp.mean(o * o, axis=0, keepdims=True) + NORM_EPS) * g_ref[...] * (1.0 - lam_init)
        o_ref[pl.ds(q0, tq), :] = o.T.astype(BF16)

    if prefetch_head:
        head(0)

    def q_loop(qi, carry):
        q_tile(qi)
        return carry

    lax.fori_loop(0, nq, q_loop, 0)


def _dot_nt(a, b):
    return lax.dot_general(a, b, (((1,), (1,)), ((), ())), preferred_element_type=F32)


def _attn_lockstep_kernel(slopes_ref, q_ref, k_ref, v_ref, lq_ref, g_ref, o_ref,
                          ka_scr, vt_scr, qa_scr, s_scr, p_scr, acc_scr, *, seq, tq, tk, lam_init):
    h = pl.program_id(1)
    nk = seq // tk
    nq = seq // tq
    slope = slopes_ref[h]
    d2 = 2 * HEAD_DIM
    l1, l2, l3 = LOG2E_PARTS

    def log2e_piece(idx):
        r = idx % 3
        return jnp.where(r == 0, l1, jnp.where(r == 1, l2, l3))

    pos = lax.broadcasted_iota(jnp.int32, (seq, d2), 0)
    col = lax.broadcasted_iota(jnp.int32, (seq, d2), 1)
    rem = pos & (POS_BLOCK - 1)
    ext = jnp.where(col < 6, log2e_piece(col),
                    jnp.where(col < 9, rem.astype(F32) * slope,
                              jnp.where(col < EXT_COLS, (pos - rem).astype(F32) * slope, 0.0)))
    for var in range(2):
        ka_scr[var, :, 0:d2] = k_ref[...]
        ka_scr[var, :, d2:2 * d2] = (ext if var == 0 else -ext).astype(BF16)
    ones_rows = jnp.where(lax.broadcasted_iota(jnp.int32, (VT_ROWS - d2, tk), 0) == 0, 1.0, 0.0).astype(BF16)
    for j in range(nk):
        vt_scr[j, 0:d2, :] = v_ref[j * tk:(j + 1) * tk, :].astype(F32).T.astype(BF16)
        vt_scr[j, d2:VT_ROWS, :] = ones_rows

    lq = lq_ref[...]
    lam = (jnp.exp(jnp.sum(lq[0:1, :] * lq[1:2, :], axis=1, keepdims=True))
           - jnp.exp(jnp.sum(lq[2:3, :] * lq[3:4, :], axis=1, keepdims=True)) + lam_init)
    rowi = lax.broadcasted_iota(jnp.int32, (d2, tq), 0)
    coli = lax.broadcasted_iota(jnp.int32, (d2, tq), 1)
    eye = (lax.broadcasted_iota(jnp.int32, (d2, d2), 0) == lax.broadcasted_iota(jnp.int32, (d2, d2), 1)).astype(BF16)

    def tile_of(i, diag):
        t = i - 1
        return jnp.where(i == 0, diag, t + jnp.where(t >= diag, 1, 0))

    def keys(var, tile):
        return ka_scr[var, pl.ds(pl.multiple_of(tile * tk, tk), tk), :]

    def scores_into(par, slot, i, diag):
        tile = tile_of(i, diag)
        ka = keys(jnp.where(tile > diag, 1, 0), tile)
        for c in range(2):
            s_scr[par, slot, c] = _dot(ka, qa_scr[par, c])

    def head(qi, par):
        q0 = pl.multiple_of(qi * tq, tq)
        qt = _dot_nt(eye, q_ref[pl.ds(q0, tq), :])
        posq = q0 + coli
        remq = posq & (POS_BLOCK - 1)
        ext_t = jnp.where(rowi < 3, -(remq.astype(F32) * slope),
                          jnp.where(rowi < 6, -((posq - remq).astype(F32) * slope),
                                    jnp.where(rowi < EXT_COLS, log2e_piece(rowi), 0.0))).astype(BF16)
        for c in range(2):
            qa_scr[par, c, 0:d2, :] = jnp.where((rowi >= c * HEAD_DIM) & (rowi < (c + 1) * HEAD_DIM),
                                                qt, 0.0).astype(BF16)
            qa_scr[par, c, d2:2 * d2, :] = ext_t
        diag = q0 // tk
        for c in range(2):
            s_scr[par, 0, c] = jnp.minimum(_dot(keys(0, diag), qa_scr[par, c]), _dot(keys(1, diag), qa_scr[par, c]))
        scores_into(par, 1, 1, diag)

    def q_tile(qi, par):
        q0 = pl.multiple_of(qi * tq, tq)
        diag = q0 // tk
        head(qi, par)
        yield

        def softmax_step(slot, c, m):
            s = s_scr[par, slot, c]
            mn = jnp.max(s, axis=0, keepdims=True)
            al = None
            if m is not None:
                mn = jnp.maximum(m, mn)
                al = jnp.exp2(m - mn)
            p_scr[par, slot, c] = jnp.exp2(s - mn).astype(BF16)
            return mn, al

        def finish(slot, i, al):
            vt = vt_scr[tile_of(i, diag)]
            for c in range(2):
                acc_scr[par, c] = al[c] * acc_scr[par, c] + _dot(vt, p_scr[par, slot, c])

        def step(i, cur, m0, m1, al, last=False):
            nxt = 1 - cur
            finish(nxt, i - 1, al)
            m0, al0 = softmax_step(cur, 0, m0)
            m1, al1 = softmax_step(cur, 1, m1)
            if not last:
                scores_into(par, nxt, i + 1, diag)
            return m0, m1, (al0, al1)

        m0, _ = softmax_step(0, 0, None)
        m1, _ = softmax_step(0, 1, None)
        acc_scr[par] = jnp.zeros_like(acc_scr[par])
        al = (jnp.ones((1, tq), F32),) * 2
        yield
        for i in range(1, nk):
            m0, m1, al = step(i, i % 2, m0, m1, al, last=(i == nk - 1))
            yield
        finish(1, nk - 1, al)
        yield
        a0 = acc_scr[par, 0]
        a1 = acc_scr[par, 1]
        o = a0[0:d2, :] / a0[d2:d2 + 1, :] - lam * (a1[0:d2, :] / a1[d2:d2 + 1, :])
        o = o * lax.rsqrt(jnp.mean(o * o, axis=0, keepdims=True) + NORM_EPS) * g_ref[...] * (1.0 - lam_init)
        o_ref[pl.ds(q0, tq), :] = o.T.astype(BF16)

    def q_loop(j, carry):
        for _ in itertools.zip_longest(q_tile(2 * j, 0), q_tile(2 * j + 1, 1)):
            pass
        return carry

    lax.fori_loop(0, nq // 2, q_loop, 0)


def _attention(qkv, batch, seq, slopes, lq, g_col, lam_init):
    t = qkv.shape[0]
    tk = ATT_TK
    col = lambda off: pl.BlockSpec((seq, 2 * HEAD_DIM), lambda b, h: (b, off + h))
    if seq // tk >= ATT_PREFETCH_MIN_STEPS:
        tq = ATT_TQ
        body = _attn_kernel
        scratch = [pltpu.VMEM((seq, KA_WIDTH), BF16),
                   pltpu.VMEM((seq // tk, VT_ROWS, tk), BF16),
                   pltpu.VMEM((4, KA_WIDTH, tq), BF16),
                   pltpu.VMEM((2, 2, tk, tq), F32),
                   pltpu.VMEM((2, 2, tk, tq), BF16),
                   pltpu.VMEM((2, VT_ROWS, tq), F32)]
    else:
        tq = ATT_TQ_PLAIN
        body = _attn_lockstep_kernel
        scratch = [pltpu.VMEM((2, seq, KA_WIDTH), BF16),
                   pltpu.VMEM((seq // tk, VT_ROWS, tk), BF16),
                   pltpu.VMEM((2, 2, KA_WIDTH, tq), BF16),
                   pltpu.VMEM((2, 2, 2, tk, tq), F32),
                   pltpu.VMEM((2, 2, 2, tk, tq), BF16),
                   pltpu.VMEM((2, 2, VT_ROWS, tq), F32)]
    kernel = functools.partial(body, seq=seq, tq=tq, tk=tk, lam_init=lam_init)
    return pl.pallas_call(
        kernel,
        grid=(batch, ATT_HEADS),
        in_specs=[pl.BlockSpec(memory_space=pltpu.SMEM), col(0), col(ATT_HEADS), col(2 * ATT_HEADS),
                  _full(lq.shape), _full(g_col.shape)],
        out_specs=col(0),
        out_shape=jax.ShapeDtypeStruct((t, ATT_WIDTH), BF16),
        scratch_shapes=scratch,
        compiler_params=_cparams(2),
        name="diff_attention",
    )(slopes, qkv, qkv, qkv, lq, g_col)


def _snap(v):
    r = round(v)
    return float(r) if abs(v - r) < 1e-12 else v


def _fourier_kernel(x_ref, fc_ref, fs_ref, twc_ref, tws_ref, cc_ref, sc_ref, o_ref, *, n2, scale):
    x = x_ref[...]
    ar = _dot(fc_ref[...], x)
    ai = _dot(fs_ref[...], x)
    c = twc_ref[...]
    s = tws_ref[...]
    br = ar * c - ai * s
    bi = -(ar * s + ai * c)
    cc = cc_ref[...]
    sc = sc_ref[...]
    for k2 in range(n2):
        xr = None
        xi = None
        for m in range(n2):
            ph = (m * k2) % n2
            cp = _snap(math.cos(2.0 * math.pi * ph / n2))
            sp = _snap(math.sin(2.0 * math.pi * ph / n2))
            brn = br[:, m * F_WIDTH:(m + 1) * F_WIDTH]
            bin_ = bi[:, m * F_WIDTH:(m + 1) * F_WIDTH]
            tr = brn * cp + bin_ * sp
            ti = bin_ * cp - brn * sp
            xr = tr if xr is None else xr + tr
            xi = ti if xi is None else xi + ti
        out = (_dot(xr.astype(BF16), cc) + _dot(xi.astype(BF16), sc)) * scale
        o_ref[k2] = out.astype(BF16)


def _fourier(fx, batch, seq, fc, fs, cc, sc):
    n1 = DFT_N1
    n2 = seq // n1
    tm = DFT_TM
    x2 = fx.reshape(batch, n1, n2 * F_WIDTH)
    k1 = lax.broadcasted_iota(jnp.int32, (n1, n2 * F_WIDTH), 0)
    m = lax.broadcasted_iota(jnp.int32, (n1, n2 * F_WIDTH), 1) // F_WIDTH
    ang = ((k1 * m) % seq).astype(F32) * (2.0 * math.pi / seq)
    twc = jnp.cos(ang)
    tws = jnp.sin(ang)
    kernel = functools.partial(_fourier_kernel, n2=n2, scale=1.0 / math.sqrt(seq * F_GW))
    out = pl.pallas_call(
        kernel,
        grid=(n1 // tm, batch),
        in_specs=[pl.BlockSpec((None, n1, n2 * F_WIDTH), lambda i, b: (b, 0, 0)),
                  pl.BlockSpec((tm, n1), lambda i, b: (i, 0)),
                  pl.BlockSpec((tm, n1), lambda i, b: (i, 0)),
                  pl.BlockSpec((tm, n2 * F_WIDTH), lambda i, b: (i, 0)),
                  pl.BlockSpec((tm, n2 * F_WIDTH), lambda i, b: (i, 0)),
                  _full(cc.shape), _full(sc.shape)],
        out_specs=pl.BlockSpec((None, n2, tm, F_WIDTH), lambda i, b: (b, 0, i, 0)),
        out_shape=jax.ShapeDtypeStruct((batch, n2, n1, F_WIDTH), BF16),
        compiler_params=_cparams(2),
        name="fourier",
    )(x2, fc, fs, twc, tws, cc, sc)
    return out.reshape(batch * seq, F_WIDTH)


def _block_diag(w):
    eye = jnp.eye(RG_HEADS, dtype=w.dtype)
    return (eye[:, None, :, None] * w[:, :, None, :]).reshape(RG_WIDTH, RG_WIDTH)


def _dft_tables():
    n = lax.broadcasted_iota(jnp.int32, (DFT_N1, DFT_N1), 0)
    k = lax.broadcasted_iota(jnp.int32, (DFT_N1, DFT_N1), 1)
    ang = ((n * k) % DFT_N1).astype(F32) * (2.0 * math.pi / DFT_N1)
    fc = jnp.cos(ang).astype(BF16)
    fs = jnp.sin(ang).astype(BF16)
    j = lax.broadcasted_iota(jnp.int32, (F_WIDTH, F_WIDTH), 0)
    jp = lax.broadcasted_iota(jnp.int32, (F_WIDTH, F_WIDTH), 1)
    same = (j // F_GW) == (jp // F_GW)
    ang_c = (((j % F_GW) * (jp % F_GW)) % F_GW).astype(F32) * (2.0 * math.pi / F_GW)
    cc = jnp.where(same, jnp.cos(ang_c), 0.0).astype(BF16)
    sc = jnp.where(same, jnp.sin(ang_c), 0.0).astype(BF16)
    return fc, fs, cc, sc


def _trunk(x, batch, seq, p, tables):
    fc, fs, cc, sc = tables
    slopes = 2.0 ** (-8.0 * jnp.arange(1, ATT_HEADS + 1, dtype=F32) / ATT_HEADS)
    for l in range(DEPTH):
        lam_init = 0.8 - 0.6 * math.exp(-0.3 * l)
        x1, rxg, qkv, fx = _ffn_inproj(x, p["ffn1_wg"][l], p["ffn1_wu"][l], p["ffn1_wd"][l],
                                       p["ln_g"][l, 0:1], p["ln_b"][l, 0:1], p["w_in"][l])
        hf, hb = _rglru(rxg, batch, seq, p["conv_w"][l], p["conv_b"][l], p["wgate"][l], p["bgate"][l],
                        p["rg_lambda"][l])
        yb = _attention(qkv, batch, seq, slopes, p["lambda_qk"][l], p["subln_g"][l], lam_init)
        yc = _fourier(fx, batch, seq, fc, fs, cc, sc)
        x = _outproj_ffn(x1, hf, hb, rxg, yb, yc, p["w_out"][l], p["ln_g"][l, 1:2], p["ln_b"][l, 1:2],
                         p["ffn2_wg"][l], p["ffn2_wu"][l], p["ffn2_wd"][l], p["ln_g"][l, 2:3], p["ln_b"][l, 2:3])
    return x


def kernel(x_prompt, x_sample, ln_g, ln_b, ffn1_wg, ffn1_wu, ffn1_wd, ffn2_wg, ffn2_wu, ffn2_wd, w_in, conv_w, conv_b,
           rg_wa, rg_ba, rg_wx, rg_bx, rg_lambda, lambda_qk, subln_g, w_out):
    wgate = jnp.stack([
        jnp.stack([jnp.concatenate([_block_diag(rg_wa[l, d]), _block_diag(rg_wx[l, d])], axis=1)
                   for d in range(2)]) for l in range(DEPTH)]).astype(BF16)
    bgate = jnp.concatenate([rg_ba, rg_bx], axis=-1)[:, :, None, :]
    p = dict(
        ln_g=ln_g, ln_b=ln_b,
        ffn1_wg=ffn1_wg.astype(BF16), ffn1_wu=ffn1_wu.astype(BF16), ffn1_wd=ffn1_wd.astype(BF16),
        ffn2_wg=ffn2_wg.astype(BF16), ffn2_wu=ffn2_wu.astype(BF16), ffn2_wd=ffn2_wd.astype(BF16),
        w_in=w_in.astype(BF16), w_out=w_out.astype(BF16),
        conv_w=conv_w, conv_b=conv_b[:, None, :],
        wgate=wgate, bgate=bgate, rg_lambda=rg_lambda[:, :, None, :],
        lambda_qk=lambda_qk, subln_g=subln_g[:, :, None],
    )
    tables = _dft_tables()
    outs = []
    for x in (x_prompt, x_sample):
        batch, seq, _ = x.shape
        y = _trunk(x.reshape(batch * seq, D_MODEL), batch, seq, p, tables)
        outs.append(y.reshape(batch, seq, D_MODEL))
    return tuple(outs)
```

```python
import functools
import itertools
import math

import numpy as np
import jax
import jax.numpy as jnp
from jax import lax
from jax.experimental import pallas as pl
from jax.experimental.pallas import tpu as pltpu

D_MODEL = 1024
DEPTH = 2
D_FF = 2816
RG_WIDTH = 256
RG_HEADS = 4
RG_BW = 64
RG_C = 8.0
CONV_W = 4
ATT_HEADS = 4
HEAD_DIM = 64
ATT_WIDTH = 512
F_WIDTH = 256
F_GROUPS = 4
F_GW = 64
IN_WIDTH = 2304
ALPHA = (2.0 * DEPTH) ** 0.25
LN_EPS = 1e-5
NORM_EPS = 1e-5

BF16 = jnp.bfloat16
F32 = jnp.float32

VMEM_LIMIT_BYTES = 56 * 1024 * 1024
SUBLANES = 8

TOKEN_TILE = 1024
SUB_TILE = 512
TOKEN_VMEM_LIMIT_BYTES = 62 * 1024 * 1024
FF_CHUNKS = ((0, 1024), (1024, 2048), (2048, 2816))
SCAN_CHUNK = 1024
ATT_TQ = 256
ATT_TQ_PLAIN = 512
ATT_TK = 512
ATT_PREFETCH_MIN_STEPS = 16
POS_BLOCK = 256
KA_WIDTH = 256
Q_PRESCALE = HEAD_DIM ** -0.5 * math.log2(math.e)
VT_ROWS = 144
DFT_N1 = 1024
DFT_TM = 256


def _cparams(n_axes, vmem_limit_bytes=VMEM_LIMIT_BYTES):
    return pltpu.CompilerParams(
        dimension_semantics=("arbitrary",) * n_axes,
        vmem_limit_bytes=vmem_limit_bytes,
    )


def _dot(a, b):
    return jnp.dot(a, b, preferred_element_type=F32)


def _layer_norm(y, g, b):
    mu = jnp.mean(y, axis=-1, keepdims=True)
    yc = y - mu
    var = jnp.mean(yc * yc, axis=-1, keepdims=True)
    return yc * lax.rsqrt(var + LN_EPS) * g + b


def _swiglu(xb, wg_ref, wu_ref, wd_ref):
    hs = []
    for lo, hi in FF_CHUNKS:
        hg = _dot(xb, wg_ref[:, lo:hi])
        hu = _dot(xb, wu_ref[:, lo:hi])
        hs.append((hg * jax.nn.sigmoid(hg) * hu).astype(BF16))
    return _dot(jnp.concatenate(hs, axis=1), wd_ref[...])


def _ffn_inproj_kernel(x_ref, wg_ref, wu_ref, wd_ref, g_ref, b_ref, win_ref,
                       xo_ref, rxg_ref, qkv_ref, fx_ref):
    q_lo, k_lo, v_hi = 2 * RG_WIDTH, 2 * RG_WIDTH + ATT_WIDTH, 2 * RG_WIDTH + 3 * ATT_WIDTH
    subs = [slice(r0, r0 + SUB_TILE) for r0 in range(0, x_ref.shape[0], SUB_TILE)]
    ys = []
    for rows in subs:
        x = x_ref[rows, :]
        ys.append(ALPHA * x + 0.5 * _swiglu(x.astype(BF16), wg_ref, wu_ref, wd_ref))
    for rows, y in zip(subs, ys):
        x1 = _layer_norm(y, g_ref[...], b_ref[...])
        xo_ref[rows, :] = x1
        xb = x1.astype(BF16)
        rxg_ref[rows, :] = _dot(xb, win_ref[:, 0:q_lo])
        qkv_ref[rows, 0:ATT_WIDTH] = (_dot(xb, win_ref[:, q_lo:k_lo]) * Q_PRESCALE).astype(BF16)
        qkv_ref[rows, ATT_WIDTH:3 * ATT_WIDTH] = _dot(xb, win_ref[:, k_lo:v_hi]).astype(BF16)
        fx_ref[rows, :] = _dot(xb, win_ref[:, v_hi:IN_WIDTH]).astype(BF16)


def _full(shape):
    return pl.BlockSpec(shape, lambda *_: (0,) * len(shape))


def _ffn_inproj(x, wg, wu, wd, g, b, win):
    t = x.shape[0]
    tm = TOKEN_TILE
    row = lambda w: pl.BlockSpec((tm, w), lambda i: (i, 0))
    return pl.pallas_call(
        _ffn_inproj_kernel,
        grid=(t // tm,),
        in_specs=[row(D_MODEL), _full(wg.shape), _full(wu.shape), _full(wd.shape),
                  _full(g.shape), _full(b.shape), _full(win.shape)],
        out_specs=[row(D_MODEL), row(2 * RG_WIDTH), row(3 * ATT_WIDTH), row(F_WIDTH)],
        out_shape=[jax.ShapeDtypeStruct((t, D_MODEL), F32),
                   jax.ShapeDtypeStruct((t, 2 * RG_WIDTH), F32),
                   jax.ShapeDtypeStruct((t, 3 * ATT_WIDTH), BF16),
                   jax.ShapeDtypeStruct((t, F_WIDTH), BF16)],
        compiler_params=_cparams(1, TOKEN_VMEM_LIMIT_BYTES),
        name="ffn_inproj",
    )(x, wg, wu, wd, g, b, win)


def _outproj_ffn_kernel(x_ref, hf_ref, hb_ref, rg_ref, yb_ref, yc_ref, wo_ref, g1_ref, b1_ref,
                        wg_ref, wu_ref, wd_ref, g2_ref, b2_ref, o_ref):
    subs = [slice(r0, r0 + SUB_TILE) for r0 in range(0, x_ref.shape[0], SUB_TILE)]
    x2s = []
    for rows in subs:
        ya = (jax.nn.gelu(rg_ref[rows, :], approximate=True) * (hf_ref[rows, :] + hb_ref[rows, :])).astype(BF16)
        mix = _dot(jnp.concatenate([ya, yb_ref[rows, :], yc_ref[rows, :]], axis=1), wo_ref[...])
        x2s.append(_layer_norm(ALPHA * x_ref[rows, :] + mix, g1_ref[...], b1_ref[...]))
    ys = [ALPHA * x2 + 0.5 * _swiglu(x2.astype(BF16), wg_ref, wu_ref, wd_ref) for x2 in x2s]
    for rows, y in zip(subs, ys):
        o_ref[rows, :] = _layer_norm(y, g2_ref[...], b2_ref[...])


def _outproj_ffn(x, hf, hb, rxg, yb, yc, wo, g1, b1, wg, wu, wd, g2, b2):
    t = x.shape[0]
    tm = TOKEN_TILE
    row = lambda w: pl.BlockSpec((tm, w), lambda i: (i, 0))
    rgate = pl.BlockSpec((tm, RG_WIDTH), lambda i: (i, 1))
    return pl.pallas_call(
        _outproj_ffn_kernel,
        grid=(t // tm,),
        in_specs=[row(D_MODEL), row(RG_WIDTH), row(RG_WIDTH), rgate, row(ATT_WIDTH), row(F_WIDTH),
                  _full(wo.shape), _full(g1.shape), _full(b1.shape),
                  _full(wg.shape), _full(wu.shape), _full(wd.shape), _full(g2.shape), _full(b2.shape)],
        out_specs=row(D_MODEL),
        out_shape=jax.ShapeDtypeStruct((t, D_MODEL), F32),
        compiler_params=_cparams(1, TOKEN_VMEM_LIMIT_BYTES),
        name="outproj_ffn",
    )(x, hf, hb, rxg, yb, yc, wo, g1, b1, wg, wu, wd, g2, b2)


def _rglru_direction(x_ref, prev_ref, next_ref, chunk, n_chunks, d, reverse,
                     cw_ref, cb_ref, wgate_ref, bgate_ref, lam_ref,
                     a_scr, u_scr, carry_scr, h_ref):
    tc = x_ref.shape[0]
    n_groups = tc // SUBLANES
    grp = (n_groups, SUBLANES, RG_WIDTH)
    rig = lax.broadcasted_iota(jnp.int32, (1, SUBLANES, 1), 1)
    x = x_ref[...].reshape(grp)
    prev = (prev_ref[...] * jnp.where(chunk > 0, 1.0, 0.0)).reshape(1, SUBLANES, RG_WIDTH)
    nxt = (next_ref[...] * jnp.where(chunk < n_chunks - 1, 1.0, 0.0)).reshape(1, SUBLANES, RG_WIDTH)

    def delayed(k):
        y = pltpu.roll(x, k, 1)
        before = jnp.concatenate([pltpu.roll(prev, k, 1), y[:-1]], axis=0)
        return jnp.where(rig >= k, y, before)

    ahead = pltpu.roll(x, SUBLANES - 1, 1)
    after = jnp.concatenate([ahead[1:], pltpu.roll(nxt, SUBLANES - 1, 1)], axis=0)
    xp1 = jnp.where(rig < SUBLANES - 1, ahead, after)
    cw = cw_ref[...]
    xc = cb_ref[...] + delayed(2) * cw[0:1, :] + delayed(1) * cw[1:2, :] + x * cw[2:3, :] + xp1 * cw[3:4, :]
    xc = xc.reshape(tc, RG_WIDTH)

    gates = _dot(xc.astype(BF16), wgate_ref[d]) + bgate_ref[d]
    r = jax.nn.sigmoid(gates[:, 0:RG_WIDTH])
    i = jax.nn.sigmoid(gates[:, RG_WIDTH:2 * RG_WIDTH])
    nlam = -lam_ref[d]
    softplus = jnp.maximum(nlam, 0.0) + jnp.log1p(jnp.exp(-jnp.abs(nlam)))
    log_a = (-RG_C * softplus) * r
    a = jnp.exp(log_a)
    th = jnp.tanh(log_a)
    u = jnp.sqrt(-2.0 * th / (1.0 - th)) * (i * xc)

    u = u.reshape(grp)
    a = a.reshape(grp)
    for k in (1, 2, 4):
        if reverse:
            keep = rig < SUBLANES - k
            shift = SUBLANES - k
        else:
            keep = rig >= k
            shift = k
        us = jnp.where(keep, pltpu.roll(u, shift, 1), 0.0)
        as_ = jnp.where(keep, pltpu.roll(a, shift, 1), 1.0)
        u = u + a * us
        a = a * as_
    a_scr[...] = a.reshape(tc, RG_WIDTH)
    u_scr[...] = u.reshape(tc, RG_WIDTH)


    def body(g, h):
        gi = n_groups - 1 - g if reverse else g
        r0 = pl.multiple_of(gi * SUBLANES, SUBLANES)
        hg = u_scr[pl.ds(r0, SUBLANES), :] + a_scr[pl.ds(r0, SUBLANES), :] * h
        h_ref[pl.ds(r0, SUBLANES), :] = hg
        edge = hg[0:1, :] if reverse else hg[SUBLANES - 1:SUBLANES, :]
        return jnp.broadcast_to(edge, (SUBLANES, RG_WIDTH))

    carry_scr[...] = lax.fori_loop(0, n_groups, body, carry_scr[...], unroll=8)


def _rglru_kernel(xf_ref, pf_ref, nf_ref, xb_ref, pb_ref, nb_ref,
                  cw_ref, cb_ref, wgate_ref, bgate_ref, lam_ref,
                  hf_ref, hb_ref, af_scr, uf_scr, ab_scr, ub_scr, cf_scr, cbk_scr, *, n_chunks):
    c = pl.program_id(1)

    @pl.when(c == 0)
    def _():
        cf_scr[...] = jnp.zeros_like(cf_scr)
        cbk_scr[...] = jnp.zeros_like(cbk_scr)

    _rglru_direction(xf_ref, pf_ref, nf_ref, c, n_chunks, 0, False,
                     cw_ref, cb_ref, wgate_ref, bgate_ref, lam_ref, af_scr, uf_scr, cf_scr, hf_ref)
    _rglru_direction(xb_ref, pb_ref, nb_ref, n_chunks - 1 - c, n_chunks, 1, True,
                     cw_ref, cb_ref, wgate_ref, bgate_ref, lam_ref, ab_scr, ub_scr, cbk_scr, hb_ref)


def _rglru(rxg, batch, seq, cw, cb, wgate, bgate, lam):
    t = rxg.shape[0]
    tc = SCAN_CHUNK
    nch = seq // tc
    hb8 = tc // SUBLANES
    last8 = t // SUBLANES - 1

    def main_f(b, c):
        return (b * nch + c, 0)

    def prev_f(b, c):
        return (jnp.maximum((b * nch + c) * hb8 - 1, 0), 0)

    def next_f(b, c):
        return (jnp.minimum((b * nch + c + 1) * hb8, last8), 0)

    def main_b(b, c):
        return (b * nch + (nch - 1 - c), 0)

    def prev_b(b, c):
        return (jnp.maximum((b * nch + (nch - 1 - c)) * hb8 - 1, 0), 0)

    def next_b(b, c):
        return (jnp.minimum((b * nch + (nch - 1 - c) + 1) * hb8, last8), 0)

    blk = lambda f: pl.BlockSpec((tc, RG_WIDTH), f)
    halo = lambda f: pl.BlockSpec((SUBLANES, RG_WIDTH), f)
    return pl.pallas_call(
        functools.partial(_rglru_kernel, n_chunks=nch),
        grid=(batch, nch),
        in_specs=[blk(main_f), halo(prev_f), halo(next_f), blk(main_b), halo(prev_b), halo(next_b),
                  _full(cw.shape), _full(cb.shape), _full(wgate.shape), _full(bgate.shape), _full(lam.shape)],
        out_specs=[blk(main_f), blk(main_b)],
        out_shape=[jax.ShapeDtypeStruct((t, RG_WIDTH), F32), jax.ShapeDtypeStruct((t, RG_WIDTH), F32)],
        scratch_shapes=[pltpu.VMEM((tc, RG_WIDTH), F32)] * 4 + [pltpu.VMEM((SUBLANES, RG_WIDTH), F32)] * 2,
        compiler_params=_cparams(2),
        name="rglru",
    )(rxg, rxg, rxg, rxg, rxg, rxg, cw, cb, wgate, bgate, lam)


def _bf16_split3(x):
    out = []
    for _ in range(3):
        hi = float(np.asarray(x, np.float32).astype(jnp.bfloat16).astype(np.float32))
        out.append(hi)
        x = x - hi
    return out


LOG2E_PARTS = _bf16_split3(math.log2(math.e))
EXT_COLS = 12


def _attn_kernel(slopes_ref, q_ref, k_ref, v_ref, lq_ref, g_ref, o_ref, ka_scr, vt_scr, qa_scr, s_scr, p_scr, acc_scr,
                 *, seq, tq, tk, lam_init):
    h = pl.program_id(1)
    nk = seq // tk
    nq = seq // tq
    prefetch_head = nk >= ATT_PREFETCH_MIN_STEPS
    slope = slopes_ref[h]
    d2 = 2 * HEAD_DIM
    l1, l2, l3 = LOG2E_PARTS

    def log2e_piece(idx):
        r = idx % 3
        return jnp.where(r == 0, l1, jnp.where(r == 1, l2, l3))

    ka_scr[:, 0:d2] = k_ref[...]
    pos = lax.broadcasted_iota(jnp.int32, (seq, d2), 0)
    col = lax.broadcasted_iota(jnp.int32, (seq, d2), 1)
    rem = pos & (POS_BLOCK - 1)
    ext = jnp.where(col < 6, log2e_piece(col),
                    jnp.where(col < 9, rem.astype(F32) * slope,
                              jnp.where(col < EXT_COLS, (pos - rem).astype(F32) * slope, 0.0)))
    ka_scr[:, d2:2 * d2] = ext.astype(BF16)
    ones_rows = jnp.where(lax.broadcasted_iota(jnp.int32, (VT_ROWS - d2, tk), 0) == 0, 1.0, 0.0).astype(BF16)
    for j in range(nk):
        vt_scr[j, 0:d2, :] = v_ref[j * tk:(j + 1) * tk, :].astype(F32).T.astype(BF16)
        vt_scr[j, d2:VT_ROWS, :] = ones_rows

    lq = lq_ref[...]
    lam = (jnp.exp(jnp.sum(lq[0:1, :] * lq[1:2, :], axis=1, keepdims=True))
           - jnp.exp(jnp.sum(lq[2:3, :] * lq[3:4, :], axis=1, keepdims=True)) + lam_init)
    rowi = lax.broadcasted_iota(jnp.int32, (d2, tq), 0)
    coli = lax.broadcasted_iota(jnp.int32, (d2, tq), 1)

    def tile_of(i, diag):
        t = i - 1
        return jnp.where(i == 0, diag, t + jnp.where(t >= diag, 1, 0))

    def keys(tile):
        return ka_scr[pl.ds(pl.multiple_of(tile * tk, tk), tk), :]

    def scores_into(slot, i, diag):
        tile = tile_of(i, diag)
        ka = keys(tile)
        var = jnp.where(tile > diag, 1, 0)
        for c in range(2):
            s_scr[slot, c] = _dot(ka, qa_scr[2 * var + c])

    def head(qi):
        q0 = pl.multiple_of(qi * tq, tq)
        qt = q_ref[pl.ds(q0, tq), :].astype(F32).T
        posq = q0 + coli
        remq = posq & (POS_BLOCK - 1)
        ext_t = jnp.where(rowi < 3, -(remq.astype(F32) * slope),
                          jnp.where(rowi < 6, -((posq - remq).astype(F32) * slope),
                                    jnp.where(rowi < EXT_COLS, log2e_piece(rowi), 0.0)))
        for c in range(2):
            qc_t = jnp.where((rowi >= c * HEAD_DIM) & (rowi < (c + 1) * HEAD_DIM), qt, 0.0).astype(BF16)
            for var in range(2):
                qa_scr[2 * var + c, 0:d2, :] = qc_t
                qa_scr[2 * var + c, d2:2 * d2, :] = (ext_t if var == 0 else -ext_t).astype(BF16)
        diag = q0 // tk
        kd = keys(diag)
        for c in range(2):
            s_scr[0, c] = jnp.minimum(_dot(kd, qa_scr[c]), _dot(kd, qa_scr[2 + c]))
        scores_into(1, 1, diag)

    def q_tile(qi):
        q0 = pl.multiple_of(qi * tq, tq)
        diag = q0 // tk
        if not prefetch_head:
            head(qi)

        def softmax_step(slot, c, m):
            s = s_scr[slot, c]
            mn = jnp.max(s, axis=0, keepdims=True)
            al = None
            if m is not None:
                mn = jnp.maximum(m, mn)
                al = jnp.exp2(m - mn)
            p_scr[slot, c] = jnp.exp2(s - mn).astype(BF16)
            return mn, al

        def finish(slot, i, al):
            vt = vt_scr[tile_of(i, diag)]
            for c in range(2):
                acc_scr[c] = al[c] * acc_scr[c] + _dot(vt, p_scr[slot, c])

        def step(i, cur, m0, m1, al, last=False):
            nxt = 1 - cur
            finish(nxt, i - 1, al)
            m0, al0 = softmax_step(cur, 0, m0)
            m1, al1 = softmax_step(cur, 1, m1)
            if not last:
                scores_into(nxt, i + 1, diag)
            return m0, m1, (al0, al1)

        m0, _ = softmax_step(0, 0, None)
        m1, _ = softmax_step(0, 1, None)
        acc_scr[...] = jnp.zeros_like(acc_scr)
        al = (jnp.ones((1, tq), F32),) * 2
        for i in range(1, nk):
            m0, m1, al = step(i, i % 2, m0, m1, al, last=(i == nk - 1))
        finish(1, nk - 1, al)
        if prefetch_head:
            head(jnp.minimum(qi + 1, nq - 1))
        a0 = acc_scr[0]
        a1 = acc_scr[1]
        o = a0[0:d2, :] / a0[d2:d2 + 1, :] - lam * (a1[0:d2, :] / a1[d2:d2 + 1, :])
        o = o * lax.rsqrt(jnp.mean(o * o, axis=0, keepdims=True) + NORM_EPS) * g_ref[...] * (1.0 - lam_init)
        o_ref[pl.ds(q0, tq), :] = o.T.astype(BF16)

    if prefetch_head:
        head(0)

    def q_loop(qi, carry):
        q_tile(qi)
        return carry

    lax.fori_loop(0, nq, q_loop, 0)


def _dot_nt(a, b):
    return lax.dot_general(a, b, (((1,), (1,)), ((), ())), preferred_element_type=F32)


def _attn_lockstep_kernel(slopes_ref, q_ref, k_ref, v_ref, lq_ref, g_ref, o_ref,
                          ka_scr, vt_scr, qa_scr, s_scr, p_scr, acc_scr, *, seq, tq, tk, lam_init):
    h = pl.program_id(1)
    nk = seq // tk
    nq = seq // tq
    slope = slopes_ref[h]
    d2 = 2 * HEAD_DIM
    l1, l2, l3 = LOG2E_PARTS

    def log2e_piece(idx):
        r = idx % 3
        return jnp.where(r == 0, l1, jnp.where(r == 1, l2, l3))

    pos = lax.broadcasted_iota(jnp.int32, (seq, d2), 0)
    col = lax.broadcasted_iota(jnp.int32, (seq, d2), 1)
    rem = pos & (POS_BLOCK - 1)
    ext = jnp.where(col < 6, log2e_piece(col),
                    jnp.where(col < 9, rem.astype(F32) * slope,
                              jnp.where(col < EXT_COLS, (pos - rem).astype(F32) * slope, 0.0)))
    for var in range(2):
        ka_scr[var, :, 0:d2] = k_ref[...]
        ka_scr[var, :, d2:2 * d2] = (ext if var == 0 else -ext).astype(BF16)
    ones_rows = jnp.where(lax.broadcasted_iota(jnp.int32, (VT_ROWS - d2, tk), 0) == 0, 1.0, 0.0).astype(BF16)
    for j in range(nk):
        vt_scr[j, 0:d2, :] = v_ref[j * tk:(j + 1) * tk, :].astype(F32).T.astype(BF16)
        vt_scr[j, d2:VT_ROWS, :] = ones_rows

    lq = lq_ref[...]
    lam = (jnp.exp(jnp.sum(lq[0:1, :] * lq[1:2, :], axis=1, keepdims=True))
           - jnp.exp(jnp.sum(lq[2:3, :] * lq[3:4, :], axis=1, keepdims=True)) + lam_init)
    rowi = lax.broadcasted_iota(jnp.int32, (d2, tq), 0)
    coli = lax.broadcasted_iota(jnp.int32, (d2, tq), 1)
    eye = (lax.broadcasted_iota(jnp.int32, (d2, d2), 0) == lax.broadcasted_iota(jnp.int32, (d2, d2), 1)).astype(BF16)

    def tile_of(i, diag):
        t = i - 1
        return jnp.where(i == 0, diag, t + jnp.where(t >= diag, 1, 0))

    def keys(var, tile):
        return ka_scr[var, pl.ds(pl.multiple_of(tile * tk, tk), tk), :]

    def scores_into(par, slot, i, diag):
        tile = tile_of(i, diag)
        ka = keys(jnp.where(tile > diag, 1, 0), tile)
        for c in range(2):
            s_scr[par, slot, c] = _dot(ka, qa_scr[par, c])

    def head(qi, par):
        q0 = pl.multiple_of(qi * tq, tq)
        qt = _dot_nt(eye, q_ref[pl.ds(q0, tq), :])
        posq = q0 + coli
        remq = posq & (POS_BLOCK - 1)
        ext_t = jnp.where(rowi < 3, -(remq.astype(F32) * slope),
                          jnp.where(rowi < 6, -((posq - remq).astype(F32) * slope),
                                    jnp.where(rowi < EXT_COLS, log2e_piece(rowi), 0.0))).astype(BF16)
        for c in range(2):
            qa_scr[par, c, 0:d2, :] = jnp.where((rowi >= c * HEAD_DIM) & (rowi < (c + 1) * HEAD_DIM),
                                                qt, 0.0).astype(BF16)
            qa_scr[par, c, d2:2 * d2, :] = ext_t
        diag = q0 // tk
        for c in range(2):
            s_scr[par, 0, c] = jnp.minimum(_dot(keys(0, diag), qa_scr[par, c]), _dot(keys(1, diag), qa_scr[par, c]))
        scores_into(par, 1, 1, diag)

    def q_tile(qi, par):
        q0 = pl.multiple_of(qi * tq, tq)
        diag = q0 // tk
        head(qi, par)
        yield

        def softmax_step(slot, c, m):
            s = s_scr[par, slot, c]
            mn = jnp.max(s, axis=0, keepdims=True)
            al = None
            if m is not None:
                mn = jnp.maximum(m, mn)
                al = jnp.exp2(m - mn)
            p_scr[par, slot, c] = jnp.exp2(s - mn).astype(BF16)
            return mn, al

        def finish(slot, i, al):
            vt = vt_scr[tile_of(i, diag)]
            for c in range(2):
                acc_scr[par, c] = al[c] * acc_scr[par, c] + _dot(vt, p_scr[par, slot, c])

        def step(i, cur, m0, m1, al, last=False):
            nxt = 1 - cur
            finish(nxt, i - 1, al)
            m0, al0 = softmax_step(cur, 0, m0)
            m1, al1 = softmax_step(cur, 1, m1)
            if not last:
                scores_into(par, nxt, i + 1, diag)
            return m0, m1, (al0, al1)

        m0, _ = softmax_step(0, 0, None)
        m1, _ = softmax_step(0, 1, None)
        acc_scr[par] = jnp.zeros_like(acc_scr[par])
        al = (jnp.ones((1, tq), F32),) * 2
        yield
        for i in range(1, nk):
            m0, m1, al = step(i, i % 2, m0, m1, al, last=(i == nk - 1))
            yield
        finish(1, nk - 1, al)
        yield
        a0 = acc_scr[par, 0]
        a1 = acc_scr[par, 1]
        o = a0[0:d2, :] / a0[d2:d2 + 1, :] - lam * (a1[0:d2, :] / a1[d2:d2 + 1, :])
        o = o * lax.rsqrt(jnp.mean(o * o, axis=0, keepdims=True) + NORM_EPS) * g_ref[...] * (1.0 - lam_init)
        o_ref[pl.ds(q0, tq), :] = o.T.astype(BF16)

    def q_loop(j, carry):
        for _ in itertools.zip_longest(q_tile(2 * j, 0), q_tile(2 * j + 1, 1)):
            pass
        return carry

    lax.fori_loop(0, nq // 2, q_loop, 0)


def _attention(qkv, batch, seq, slopes, lq, g_col, lam_init):
    t = qkv.shape[0]
    tk = ATT_TK
    col = lambda off: pl.BlockSpec((seq, 2 * HEAD_DIM), lambda b, h: (b, off + h))
    if seq // tk >= ATT_PREFETCH_MIN_STEPS:
        tq = ATT_TQ
        body = _attn_kernel
        scratch = [pltpu.VMEM((seq, KA_WIDTH), BF16),
                   pltpu.VMEM((seq // tk, VT_ROWS, tk), BF16),
                   pltpu.VMEM((4, KA_WIDTH, tq), BF16),
                   pltpu.VMEM((2, 2, tk, tq), F32),
                   pltpu.VMEM((2, 2, tk, tq), BF16),
                   pltpu.VMEM((2, VT_ROWS, tq), F32)]
    else:
        tq = ATT_TQ_PLAIN
        body = _attn_lockstep_kernel
        scratch = [pltpu.VMEM((2, seq, KA_WIDTH), BF16),
                   pltpu.VMEM((seq // tk, VT_ROWS, tk), BF16),
                   pltpu.VMEM((2, 2, KA_WIDTH, tq), BF16),
                   pltpu.VMEM((2, 2, 2, tk, tq), F32),
                   pltpu.VMEM((2, 2, 2, tk, tq), BF16),
                   pltpu.VMEM((2, 2, VT_ROWS, tq), F32)]
    kernel = functools.partial(body, seq=seq, tq=tq, tk=tk, lam_init=lam_init)
    return pl.pallas_call(
        kernel,
        grid=(batch, ATT_HEADS),
        in_specs=[pl.BlockSpec(memory_space=pltpu.SMEM), col(0), col(ATT_HEADS), col(2 * ATT_HEADS),
                  _full(lq.shape), _full(g_col.shape)],
        out_specs=col(0),
        out_shape=jax.ShapeDtypeStruct((t, ATT_WIDTH), BF16),
        scratch_shapes=scratch,
        compiler_params=_cparams(2),
        name="diff_attention",
    )(slopes, qkv, qkv, qkv, lq, g_col)


def _snap(v):
    r = round(v)
    return float(r) if abs(v - r) < 1e-12 else v


def _fourier_kernel(x_ref, fc_ref, fs_ref, twc_ref, tws_ref, cc_ref, sc_ref, o_ref, *, n2, scale):
    x = x_ref[...]
    ar = _dot(fc_ref[...], x)
    ai = _dot(fs_ref[...], x)
    c = twc_ref[...]
    s = tws_ref[...]
    br = ar * c - ai * s
    bi = -(ar * s + ai * c)
    cc = cc_ref[...]
    sc = sc_ref[...]
    for k2 in range(n2):
        xr = None
        xi = None
        for m in range(n2):
            ph = (m * k2) % n2
            cp = _snap(math.cos(2.0 * math.pi * ph / n2))
            sp = _snap(math.sin(2.0 * math.pi * ph / n2))
            brn = br[:, m * F_WIDTH:(m + 1) * F_WIDTH]
            bin_ = bi[:, m * F_WIDTH:(m + 1) * F_WIDTH]
            tr = brn * cp + bin_ * sp
            ti = bin_ * cp - brn * sp
            xr = tr if xr is None else xr + tr
            xi = ti if xi is None else xi + ti
        out = (_dot(xr.astype(BF16), cc) + _dot(xi.astype(BF16), sc)) * scale
        o_ref[k2] = out.astype(BF16)


def _fourier(fx, batch, seq, fc, fs, cc, sc):
    n1 = DFT_N1
    n2 = seq // n1
    tm = DFT_TM
    x2 = fx.reshape(batch, n1, n2 * F_WIDTH)
    k1 = lax.broadcasted_iota(jnp.int32, (n1, n2 * F_WIDTH), 0)
    m = lax.broadcasted_iota(jnp.int32, (n1, n2 * F_WIDTH), 1) // F_WIDTH
    ang = ((k1 * m) % seq).astype(F32) * (2.0 * math.pi / seq)
    twc = jnp.cos(ang)
    tws = jnp.sin(ang)
    kernel = functools.partial(_fourier_kernel, n2=n2, scale=1.0 / math.sqrt(seq * F_GW))
    out = pl.pallas_call(
        kernel,
        grid=(n1 // tm, batch),
        in_specs=[pl.BlockSpec((None, n1, n2 * F_WIDTH), lambda i, b: (b, 0, 0)),
                  pl.BlockSpec((tm, n1), lambda i, b: (i, 0)),
                  pl.BlockSpec((tm, n1), lambda i, b: (i, 0)),
                  pl.BlockSpec((tm, n2 * F_WIDTH), lambda i, b: (i, 0)),
                  pl.BlockSpec((tm, n2 * F_WIDTH), lambda i, b: (i, 0)),
                  _full(cc.shape), _full(sc.shape)],
        out_specs=pl.BlockSpec((None, n2, tm, F_WIDTH), lambda i, b: (b, 0, i, 0)),
        out_shape=jax.ShapeDtypeStruct((batch, n2, n1, F_WIDTH), BF16),
        compiler_params=_cparams(2),
        name="fourier",
    )(x2, fc, fs, twc, tws, cc, sc)
    return out.reshape(batch * seq, F_WIDTH)


def _block_diag(w):
    eye = jnp.eye(RG_HEADS, dtype=w.dtype)
    return (eye[:, None, :, None] * w[:, :, None, :]).reshape(RG_WIDTH, RG_WIDTH)


def _dft_tables():
    n = lax.broadcasted_iota(jnp.int32, (DFT_N1, DFT_N1), 0)
    k = lax.broadcasted_iota(jnp.int32, (DFT_N1, DFT_N1), 1)
    ang = ((n * k) % DFT_N1).astype(F32) * (2.0 * math.pi / DFT_N1)
    fc = jnp.cos(ang).astype(BF16)
    fs = jnp.sin(ang).astype(BF16)
    j = lax.broadcasted_iota(jnp.int32, (F_WIDTH, F_WIDTH), 0)
    jp = lax.broadcasted_iota(jnp.int32, (F_WIDTH, F_WIDTH), 1)
    same = (j // F_GW) == (jp // F_GW)
    ang_c = (((j % F_GW) * (jp % F_GW)) % F_GW).astype(F32) * (2.0 * math.pi / F_GW)
    cc = jnp.where(same, jnp.cos(ang_c), 0.0).astype(BF16)
    sc = jnp.where(same, jnp.sin(ang_c), 0.0).astype(BF16)
    return fc, fs, cc, sc


def _trunk(x, batch, seq, p, tables):
    fc, fs, cc, sc = tables
    slopes = 2.0 ** (-8.0 * jnp.arange(1, ATT_HEADS + 1, dtype=F32) / ATT_HEADS)
    for l in range(DEPTH):
        lam_init = 0.8 - 0.6 * math.exp(-0.3 * l)
        x1, rxg, qkv, fx = _ffn_inproj(x, p["ffn1_wg"][l], p["ffn1_wu"][l], p["ffn1_wd"][l],
                                       p["ln_g"][l, 0:1], p["ln_b"][l, 0:1], p["w_in"][l])
        hf, hb = _rglru(rxg, batch, seq, p["conv_w"][l], p["conv_b"][l], p["wgate"][l], p["bgate"][l],
                        p["rg_lambda"][l])
        yb = _attention(qkv, batch, seq, slopes, p["lambda_qk"][l], p["subln_g"][l], lam_init)
        yc = _fourier(fx, batch, seq, fc, fs, cc, sc)
        x = _outproj_ffn(x1, hf, hb, rxg, yb, yc, p["w_out"][l], p["ln_g"][l, 1:2], p["ln_b"][l, 1:2],
                         p["ffn2_wg"][l], p["ffn2_wu"][l], p["ffn2_wd"][l], p["ln_g"][l, 2:3], p["ln_b"][l, 2:3])
    return x


def kernel(x_prompt, x_sample, ln_g, ln_b, ffn1_wg, ffn1_wu, ffn1_wd, ffn2_wg, ffn2_wu, ffn2_wd, w_in, conv_w, conv_b,
           rg_wa, rg_ba, rg_wx, rg_bx, rg_lambda, lambda_qk, subln_g, w_out):
    wgate = jnp.stack([
        jnp.stack([jnp.concatenate([_block_diag(rg_wa[l, d]), _block_diag(rg_wx[l, d])], axis=1)
                   for d in range(2)]) for l in range(DEPTH)]).astype(BF16)
    bgate = jnp.concatenate([rg_ba, rg_bx], axis=-1)[:, :, None, :]
    p = dict(
        ln_g=ln_g, ln_b=ln_b,
        ffn1_wg=ffn1_wg.astype(BF16), ffn1_wu=ffn1_wu.astype(BF16), ffn1_wd=ffn1_wd.astype(BF16),
        ffn2_wg=ffn2_wg.astype(BF16), ffn2_wu=ffn2_wu.astype(BF16), ffn2_wd=ffn2_wd.astype(BF16),
        w_in=w_in.astype(BF16), w_out=w_out.astype(BF16),
        conv_w=conv_w, conv_b=conv_b[:, None, :],
        wgate=wgate, bgate=bgate, rg_lambda=rg_lambda[:, :, None, :],
        lambda_qk=lambda_qk, subln_g=subln_g[:, :, None],
    )
    tables = _dft_tables()
    outs = []
    for x in (x_prompt, x_sample):
        batch, seq, _ = x.shape
        y = _trunk(x.reshape(batch * seq, D_MODEL), batch, seq, p, tables)
        outs.append(y.reshape(batch, seq, D_MODEL))
    return tuple(outs)
```

```python
import functools
import itertools
import math

import numpy as np
import jax
import jax.numpy as jnp
from jax import lax
from jax.experimental import pallas as pl
from jax.experimental.pallas import tpu as pltpu

D_MODEL = 1024
DEPTH = 2
D_FF = 2816
RG_WIDTH = 256
RG_HEADS = 4
RG_BW = 64
RG_C = 8.0
CONV_W = 4
ATT_HEADS = 4
HEAD_DIM = 64
ATT_WIDTH = 512
F_WIDTH = 256
F_GROUPS = 4
F_GW = 64
IN_WIDTH = 2304
ALPHA = (2.0 * DEPTH) ** 0.25
LN_EPS = 1e-5
NORM_EPS = 1e-5

BF16 = jnp.bfloat16
F32 = jnp.float32

VMEM_LIMIT_BYTES = 56 * 1024 * 1024
SUBLANES = 8

TOKEN_TILE = 1024
SUB_TILE = 512
TOKEN_VMEM_LIMIT_BYTES = 62 * 1024 * 1024
FF_CHUNKS = ((0, 1024), (1024, 2048), (2048, 2816))
SCAN_CHUNK = 1024
ATT_TQ = 256
ATT_TQ_PLAIN = 512
ATT_TK = 512
ATT_PREFETCH_MIN_STEPS = 16
POS_BLOCK = 256
KA_WIDTH = 256
Q_PRESCALE = HEAD_DIM ** -0.5 * math.log2(math.e)
VT_ROWS = 144
DFT_N1 = 1024
DFT_TM = 256


def _cparams(n_axes, vmem_limit_bytes=VMEM_LIMIT_BYTES):
    return pltpu.CompilerParams(
        dimension_semantics=("arbitrary",) * n_axes,
        vmem_limit_bytes=vmem_limit_bytes,
    )


def _dot(a, b):
    return jnp.dot(a, b, preferred_element_type=F32)


def _layer_norm(y, g, b):
    mu = jnp.mean(y, axis=-1, keepdims=True)
    yc = y - mu
    var = jnp.mean(yc * yc, axis=-1, keepdims=True)
    return yc * lax.rsqrt(var + LN_EPS) * g + b


def _swiglu(xb, wg_ref, wu_ref, wd_ref):
    hs = []
    for lo, hi in FF_CHUNKS:
        hg = _dot(xb, wg_ref[:, lo:hi])
        hu = _dot(xb, wu_ref[:, lo:hi])
        hs.append((hg * jax.nn.sigmoid(hg) * hu).astype(BF16))
    return _dot(jnp.concatenate(hs, axis=1), wd_ref[...])


def _ffn_inproj_kernel(x_ref, wg_ref, wu_ref, wd_ref, g_ref, b_ref, win_ref,
                       xo_ref, rxg_ref, qkv_ref, fx_ref):
    q_lo, k_lo, v_hi = 2 * RG_WIDTH, 2 * RG_WIDTH + ATT_WIDTH, 2 * RG_WIDTH + 3 * ATT_WIDTH
    subs = [slice(r0, r0 + SUB_TILE) for r0 in range(0, x_ref.shape[0], SUB_TILE)]
    ys = []
    for rows in subs:
        x = x_ref[rows, :]
        ys.append(ALPHA * x + 0.5 * _swiglu(x.astype(BF16), wg_ref, wu_ref, wd_ref))
    for rows, y in zip(subs, ys):
        x1 = _layer_norm(y, g_ref[...], b_ref[...])
        xo_ref[rows, :] = x1
        xb = x1.astype(BF16)
        rxg_ref[rows, :] = _dot(xb, win_ref[:, 0:q_lo])
        qkv_ref[rows, 0:ATT_WIDTH] = (_dot(xb, win_ref[:, q_lo:k_lo]) * Q_PRESCALE).astype(BF16)
        qkv_ref[rows, ATT_WIDTH:3 * ATT_WIDTH] = _dot(xb, win_ref[:, k_lo:v_hi]).astype(BF16)
        fx_ref[rows, :] = _dot(xb, win_ref[:, v_hi:IN_WIDTH]).astype(BF16)


def _full(shape):
    return pl.BlockSpec(shape, lambda *_: (0,) * len(shape))


def _ffn_inproj(x, wg, wu, wd, g, b, win):
    t = x.shape[0]
    tm = TOKEN_TILE
    row = lambda w: pl.BlockSpec((tm, w), lambda i: (i, 0))
    return pl.pallas_call(
        _ffn_inproj_kernel,
        grid=(t // tm,),
        in_specs=[row(D_MODEL), _full(wg.shape), _full(wu.shape), _full(wd.shape),
                  _full(g.shape), _full(b.shape), _full(win.shape)],
        out_specs=[row(D_MODEL), row(2 * RG_WIDTH), row(3 * ATT_WIDTH), row(F_WIDTH)],
        out_shape=[jax.ShapeDtypeStruct((t, D_MODEL), F32),
                   jax.ShapeDtypeStruct((t, 2 * RG_WIDTH), F32),
                   jax.ShapeDtypeStruct((t, 3 * ATT_WIDTH), BF16),
                   jax.ShapeDtypeStruct((t, F_WIDTH), BF16)],
        compiler_params=_cparams(1, TOKEN_VMEM_LIMIT_BYTES),
        name="ffn_inproj",
    )(x, wg, wu, wd, g, b, win)


def _outproj_ffn_kernel(x_ref, hf_ref, hb_ref, rg_ref, yb_ref, yc_ref, wo_ref, g1_ref, b1_ref,
                        wg_ref, wu_ref, wd_ref, g2_ref, b2_ref, o_ref):
    subs = [slice(r0, r0 + SUB_TILE) for r0 in range(0, x_ref.shape[0], SUB_TILE)]
    x2s = []
    for rows in subs:
        ya = (jax.nn.gelu(rg_ref[rows, :], approximate=True) * (hf_ref[rows, :] + hb_ref[rows, :])).astype(BF16)
        mix = _dot(jnp.concatenate([ya, yb_ref[rows, :], yc_ref[rows, :]], axis=1), wo_ref[...])
        x2s.append(_layer_norm(ALPHA * x_ref[rows, :] + mix, g1_ref[...], b1_ref[...]))
    ys = [ALPHA * x2 + 0.5 * _swiglu(x2.astype(BF16), wg_ref, wu_ref, wd_ref) for x2 in x2s]
    for rows, y in zip(subs, ys):
        o_ref[rows, :] = _layer_norm(y, g2_ref[...], b2_ref[...])


def _outproj_ffn(x, hf, hb, rxg, yb, yc, wo, g1, b1, wg, wu, wd, g2, b2):
    t = x.shape[0]
    tm = TOKEN_TILE
    row = lambda w: pl.BlockSpec((tm, w), lambda i: (i, 0))
    rgate = pl.BlockSpec((tm, RG_WIDTH), lambda i: (i, 1))
    return pl.pallas_call(
        _outproj_ffn_kernel,
        grid=(t // tm,),
        in_specs=[row(D_MODEL), row(RG_WIDTH), row(RG_WIDTH), rgate, row(ATT_WIDTH), row(F_WIDTH),
                  _full(wo.shape), _full(g1.shape), _full(b1.shape),
                  _full(wg.shape), _full(wu.shape), _full(wd.shape), _full(g2.shape), _full(b2.shape)],
        out_specs=row(D_MODEL),
        out_shape=jax.ShapeDtypeStruct((t, D_MODEL), F32),
        compiler_params=_cparams(1, TOKEN_VMEM_LIMIT_BYTES),
        name="outproj_ffn",
    )(x, hf, hb, rxg, yb, yc, wo, g1, b1, wg, wu, wd, g2, b2)


def _rglru_direction(x_ref, prev_ref, next_ref, chunk, n_chunks, d, reverse,
                     cw_ref, cb_ref, wgate_ref, bgate_ref, lam_ref,
                     a_scr, u_scr, carry_scr, h_ref):
    tc = x_ref.shape[0]
    n_groups = tc // SUBLANES
    grp = (n_groups, SUBLANES, RG_WIDTH)
    rig = lax.broadcasted_iota(jnp.int32, (1, SUBLANES, 1), 1)
    x = x_ref[...].reshape(grp)
    prev = (prev_ref[...] * jnp.where(chunk > 0, 1.0, 0.0)).reshape(1, SUBLANES, RG_WIDTH)
    nxt = (next_ref[...] * jnp.where(chunk < n_chunks - 1, 1.0, 0.0)).reshape(1, SUBLANES, RG_WIDTH)

    def delayed(k):
        y = pltpu.roll(x, k, 1)
        before = jnp.concatenate([pltpu.roll(prev, k, 1), y[:-1]], axis=0)
        return jnp.where(rig >= k, y, before)

    ahead = pltpu.roll(x, SUBLANES - 1, 1)
    after = jnp.concatenate([ahead[1:], pltpu.roll(nxt, SUBLANES - 1, 1)], axis=0)
    xp1 = jnp.where(rig < SUBLANES - 1, ahead, after)
    cw = cw_ref[...]
    xc = cb_ref[...] + delayed(2) * cw[0:1, :] + delayed(1) * cw[1:2, :] + x * cw[2:3, :] + xp1 * cw[3:4, :]
    xc = xc.reshape(tc, RG_WIDTH)

    gates = _dot(xc.astype(BF16), wgate_ref[d]) + bgate_ref[d]
    r = jax.nn.sigmoid(gates[:, 0:RG_WIDTH])
    i = jax.nn.sigmoid(gates[:, RG_WIDTH:2 * RG_WIDTH])
    nlam = -lam_ref[d]
    softplus = jnp.maximum(nlam, 0.0) + jnp.log1p(jnp.exp(-jnp.abs(nlam)))
    log_a = (-RG_C * softplus) * r
    a = jnp.exp(log_a)
    th = jnp.tanh(log_a)
    u = jnp.sqrt(-2.0 * th / (1.0 - th)) * (i * xc)

    u = u.reshape(grp)
    a = a.reshape(grp)
    for k in (1, 2, 4):
        if reverse:
            keep = rig < SUBLANES - k
            shift = SUBLANES - k
        else:
            keep = rig >= k
            shift = k
        us = jnp.where(keep, pltpu.roll(u, shift, 1), 0.0)
        as_ = jnp.where(keep, pltpu.roll(a, shift, 1), 1.0)
        u = u + a * us
        a = a * as_
    a_scr[...] = a.reshape(tc, RG_WIDTH)
    u_scr[...] = u.reshape(tc, RG_WIDTH)


    def body(g, h):
        gi = n_groups - 1 - g if reverse else g
        r0 = pl.multiple_of(gi * SUBLANES, SUBLANES)
        hg = u_scr[pl.ds(r0, SUBLANES), :] + a_scr[pl.ds(r0, SUBLANES), :] * h
        h_ref[pl.ds(r0, SUBLANES), :] = hg
        edge = hg[0:1, :] if reverse else hg[SUBLANES - 1:SUBLANES, :]
        return jnp.broadcast_to(edge, (SUBLANES, RG_WIDTH))

    carry_scr[...] = lax.fori_loop(0, n_groups, body, carry_scr[...], unroll=8)


def _rglru_kernel(xf_ref, pf_ref, nf_ref, xb_ref, pb_ref, nb_ref,
                  cw_ref, cb_ref, wgate_ref, bgate_ref, lam_ref,
                  hf_ref, hb_ref, af_scr, uf_scr, ab_scr, ub_scr, cf_scr, cbk_scr, *, n_chunks):
    c = pl.program_id(1)

    @pl.when(c == 0)
    def _():
        cf_scr[...] = jnp.zeros_like(cf_scr)
        cbk_scr[...] = jnp.zeros_like(cbk_scr)

    _rglru_direction(xf_ref, pf_ref, nf_ref, c, n_chunks, 0, False,
                     cw_ref, cb_ref, wgate_ref, bgate_ref, lam_ref, af_scr, uf_scr, cf_scr, hf_ref)
    _rglru_direction(xb_ref, pb_ref, nb_ref, n_chunks - 1 - c, n_chunks, 1, True,
                     cw_ref, cb_ref, wgate_ref, bgate_ref, lam_ref, ab_scr, ub_scr, cbk_scr, hb_ref)


def _rglru(rxg, batch, seq, cw, cb, wgate, bgate, lam):
    t = rxg.shape[0]
    tc = SCAN_CHUNK
    nch = seq // tc
    hb8 = tc // SUBLANES
    last8 = t // SUBLANES - 1

    def main_f(b, c):
        return (b * nch + c, 0)

    def prev_f(b, c):
        return (jnp.maximum((b * nch + c) * hb8 - 1, 0), 0)

    def next_f(b, c):
        return (jnp.minimum((b * nch + c + 1) * hb8, last8), 0)

    def main_b(b, c):
        return (b * nch + (nch - 1 - c), 0)

    def prev_b(b, c):
        return (jnp.maximum((b * nch + (nch - 1 - c)) * hb8 - 1, 0), 0)

    def next_b(b, c):
        return (jnp.minimum((b * nch + (nch - 1 - c) + 1) * hb8, last8), 0)

    blk = lambda f: pl.BlockSpec((tc, RG_WIDTH), f)
    halo = lambda f: pl.BlockSpec((SUBLANES, RG_WIDTH), f)
    return pl.pallas_call(
        functools.partial(_rglru_kernel, n_chunks=nch),
        grid=(batch, nch),
        in_specs=[blk(main_f), halo(prev_f), halo(next_f), blk(main_b), halo(prev_b), halo(next_b),
                  _full(cw.shape), _full(cb.shape), _full(wgate.shape), _full(bgate.shape), _full(lam.shape)],
        out_specs=[blk(main_f), blk(main_b)],
        out_shape=[jax.ShapeDtypeStruct((t, RG_WIDTH), F32), jax.ShapeDtypeStruct((t, RG_WIDTH), F32)],
        scratch_shapes=[pltpu.VMEM((tc, RG_WIDTH), F32)] * 4 + [pltpu.VMEM((SUBLANES, RG_WIDTH), F32)] * 2,
        compiler_params=_cparams(2),
        name="rglru",
    )(rxg, rxg, rxg, rxg, rxg, rxg, cw, cb, wgate, bgate, lam)


def _bf16_split3(x):
    out = []
    for _ in range(3):
        hi = float(np.asarray(x, np.float32).astype(jnp.bfloat16).astype(np.float32))
        out.append(hi)
        x = x - hi
    return out


LOG2E_PARTS = _bf16_split3(math.log2(math.e))
EXT_COLS = 12


def _attn_kernel(slopes_ref, q_ref, k_ref, v_ref, lq_ref, g_ref, o_ref, ka_scr, vt_scr, qa_scr, s_scr, p_scr, acc_scr,
                 *, seq, tq, tk, lam_init):
    h = pl.program_id(1)
    nk = seq // tk
    nq = seq // tq
    prefetch_head = nk >= ATT_PREFETCH_MIN_STEPS
    slope = slopes_ref[h]
    d2 = 2 * HEAD_DIM
    l1, l2, l3 = LOG2E_PARTS

    def log2e_piece(idx):
        r = idx % 3
        return jnp.where(r == 0, l1, jnp.where(r == 1, l2, l3))

    ka_scr[:, 0:d2] = k_ref[...]
    pos = lax.broadcasted_iota(jnp.int32, (seq, d2), 0)
    col = lax.broadcasted_iota(jnp.int32, (seq, d2), 1)
    rem = pos & (POS_BLOCK - 1)
    ext = jnp.where(col < 6, log2e_piece(col),
                    jnp.where(col < 9, rem.astype(F32) * slope,
                              jnp.where(col < EXT_COLS, (pos - rem).astype(F32) * slope, 0.0)))
    ka_scr[:, d2:2 * d2] = ext.astype(BF16)
    ones_rows = jnp.where(lax.broadcasted_iota(jnp.int32, (VT_ROWS - d2, tk), 0) == 0, 1.0, 0.0).astype(BF16)
    for j in range(nk):
        vt_scr[j, 0:d2, :] = v_ref[j * tk:(j + 1) * tk, :].astype(F32).T.astype(BF16)
        vt_scr[j, d2:VT_ROWS, :] = ones_rows

    lq = lq_ref[...]
    lam = (jnp.exp(jnp.sum(lq[0:1, :] * lq[1:2, :], axis=1, keepdims=True))
           - jnp.exp(jnp.sum(lq[2:3, :] * lq[3:4, :], axis=1, keepdims=True)) + lam_init)
    rowi = lax.broadcasted_iota(jnp.int32, (d2, tq), 0)
    coli = lax.broadcasted_iota(jnp.int32, (d2, tq), 1)

    def tile_of(i, diag):
        t = i - 1
        return jnp.where(i == 0, diag, t + jnp.where(t >= diag, 1, 0))

    def keys(tile):
        return ka_scr[pl.ds(pl.multiple_of(tile * tk, tk), tk), :]

    def scores_into(slot, i, diag):
        tile = tile_of(i, diag)
        ka = keys(tile)
        var = jnp.where(tile > diag, 1, 0)
        for c in range(2):
            s_scr[slot, c] = _dot(ka, qa_scr[2 * var + c])

    def head(qi):
        q0 = pl.multiple_of(qi * tq, tq)
        qt = q_ref[pl.ds(q0, tq), :].astype(F32).T
        posq = q0 + coli
        remq = posq & (POS_BLOCK - 1)
        ext_t = jnp.where(rowi < 3, -(remq.astype(F32) * slope),
                          jnp.where(rowi < 6, -((posq - remq).astype(F32) * slope),
                                    jnp.where(rowi < EXT_COLS, log2e_piece(rowi), 0.0)))
        for c in range(2):
            qc_t = jnp.where((rowi >= c * HEAD_DIM) & (rowi < (c + 1) * HEAD_DIM), qt, 0.0).astype(BF16)
            for var in range(2):
                qa_scr[2 * var + c, 0:d2, :] = qc_t
                qa_scr[2 * var + c, d2:2 * d2, :] = (ext_t if var == 0 else -ext_t).astype(BF16)
        diag = q0 // tk
        kd = keys(diag)
        for c in range(2):
            s_scr[0, c] = jnp.minimum(_dot(kd, qa_scr[c]), _dot(kd, qa_scr[2 + c]))
        scores_into(1, 1, diag)

    def q_tile(qi):
        q0 = pl.multiple_of(qi * tq, tq)
        diag = q0 // tk
        if not prefetch_head:
            head(qi)

        def softmax_step(slot, c, m):
            s = s_scr[slot, c]
            mn = jnp.max(s, axis=0, keepdims=True)
            al = None
            if m is not None:
                mn = jnp.maximum(m, mn)
                al = jnp.exp2(m - mn)
            p_scr[slot, c] = jnp.exp2(s - mn).astype(BF16)
            return mn, al

        def finish(slot, i, al):
            vt = vt_scr[tile_of(i, diag)]
            for c in range(2):
                acc_scr[c] = al[c] * acc_scr[c] + _dot(vt, p_scr[slot, c])

        def step(i, cur, m0, m1, al, last=False):
            nxt = 1 - cur
            finish(nxt, i - 1, al)
            m0, al0 = softmax_step(cur, 0, m0)
            m1, al1 = softmax_step(cur, 1, m1)
            if not last:
                scores_into(nxt, i + 1, diag)
            return m0, m1, (al0, al1)

        m0, _ = softmax_step(0, 0, None)
        m1, _ = softmax_step(0, 1, None)
        acc_scr[...] = jnp.zeros_like(acc_scr)
        al = (jnp.ones((1, tq), F32),) * 2
        for i in range(1, nk):
            m0, m1, al = step(i, i % 2, m0, m1, al, last=(i == nk - 1))
        finish(1, nk - 1, al)
        if prefetch_head:
            head(jnp.minimum(qi + 1, nq - 1))
        a0 = acc_scr[0]
        a1 = acc_scr[1]
        o = a0[0:d2, :] / a0[d2:d2 + 1, :] - lam * (a1[0:d2, :] / a1[d2:d2 + 1, :])
        o = o * lax.rsqrt(jnp.mean(o * o, axis=0, keepdims=True) + NORM_EPS) * g_ref[...] * (1.0 - lam_init)
        o_ref[pl.ds(q0, tq), :] = o.T.astype(BF16)

    if prefetch_head:
        head(0)

    def q_loop(qi, carry):
        q_tile(qi)
        return carry

    lax.fori_loop(0, nq, q_loop, 0)


def _dot_nt(a, b):
    return lax.dot_general(a, b, (((1,), (1,)), ((), ())), preferred_element_type=F32)


def _attn_lockstep_kernel(slopes_ref, q_ref, k_ref, v_ref, lq_ref, g_ref, o_ref,
                          ka_scr, vt_scr, qa_scr, s_scr, p_scr, acc_scr, *, seq, tq, tk, lam_init):
    h = pl.program_id(1)
    nk = seq // tk
    nq = seq // tq
    slope = slopes_ref[h]
    d2 = 2 * HEAD_DIM
    l1, l2, l3 = LOG2E_PARTS

    def log2e_piece(idx):
        r = idx % 3
        return jnp.where(r == 0, l1, jnp.where(r == 1, l2, l3))

    pos = lax.broadcasted_iota(jnp.int32, (seq, d2), 0)
    col = lax.broadcasted_iota(jnp.int32, (seq, d2), 1)
    rem = pos & (POS_BLOCK - 1)
    ext = jnp.where(col < 6, log2e_piece(col),
                    jnp.where(col < 9, rem.astype(F32) * slope,
                              jnp.where(col < EXT_COLS, (pos - rem).astype(F32) * slope, 0.0)))
    for var in range(2):
        ka_scr[var, :, 0:d2] = k_ref[...]
        ka_scr[var, :, d2:2 * d2] = (ext if var == 0 else -ext).astype(BF16)
    ones_rows = jnp.where(lax.broadcasted_iota(jnp.int32, (VT_ROWS - d2, tk), 0) == 0, 1.0, 0.0).astype(BF16)
    for j in range(nk):
        vt_scr[j, 0:d2, :] = v_ref[j * tk:(j + 1) * tk, :].astype(F32).T.astype(BF16)
        vt_scr[j, d2:VT_ROWS, :] = ones_rows

    lq = lq_ref[...]
    lam = (jnp.exp(jnp.sum(lq[0:1, :] * lq[1:2, :], axis=1, keepdims=True))
           - jnp.exp(jnp.sum(lq[2:3, :] * lq[3:4, :], axis=1, keepdims=True)) + lam_init)
    rowi = lax.broadcasted_iota(jnp.int32, (d2, tq), 0)
    coli = lax.broadcasted_iota(jnp.int32, (d2, tq), 1)
    eye = (lax.broadcasted_iota(jnp.int32, (d2, d2), 0) == lax.broadcasted_iota(jnp.int32, (d2, d2), 1)).astype(BF16)

    def tile_of(i, diag):
        t = i - 1
        return jnp.where(i == 0, diag, t + jnp.where(t >= diag, 1, 0))

    def keys(var, tile):
        return ka_scr[var, pl.ds(pl.multiple_of(tile * tk, tk), tk), :]

    def scores_into(par, slot, i, diag):
        tile = tile_of(i, diag)
        ka = keys(jnp.where(tile > diag, 1, 0), tile)
        for c in range(2):
            s_scr[par, slot, c] = _dot(ka, qa_scr[par, c])

    def head(qi, par):
        q0 = pl.multiple_of(qi * tq, tq)
        qt = _dot_nt(eye, q_ref[pl.ds(q0, tq), :])
        posq = q0 + coli
        remq = posq & (POS_BLOCK - 1)
        ext_t = jnp.where(rowi < 3, -(remq.astype(F32) * slope),
                          jnp.where(rowi < 6, -((posq - remq).astype(F32) * slope),
                                    jnp.where(rowi < EXT_COLS, log2e_piece(rowi), 0.0))).astype(BF16)
        for c in range(2):
            qa_scr[par, c, 0:d2, :] = jnp.where((rowi >= c * HEAD_DIM) & (rowi < (c + 1) * HEAD_DIM),
                                                qt, 0.0).astype(BF16)
            qa_scr[par, c, d2:2 * d2, :] = ext_t
        diag = q0 // tk
        for c in range(2):
            s_scr[par, 0, c] = jnp.minimum(_dot(keys(0, diag), qa_scr[par, c]), _dot(keys(1, diag), qa_scr[par, c]))
        scores_into(par, 1, 1, diag)

    def q_tile(qi, par):
        q0 = pl.multiple_of(qi * tq, tq)
        diag = q0 // tk
        head(qi, par)
        yield

        def softmax_step(slot, c, m):
            s = s_scr[par, slot, c]
            mn = jnp.max(s, axis=0, keepdims=True)
            al = None
            if m is not None:
                mn = jnp.maximum(m, mn)
                al = jnp.exp2(m - mn)
            p_scr[par, slot, c] = jnp.exp2(s - mn).astype(BF16)
            return mn, al

        def finish(slot, i, al):
            vt = vt_scr[tile_of(i, diag)]
            for c in range(2):
                acc_scr[par, c] = al[c] * acc_scr[par, c] + _dot(vt, p_scr[par, slot, c])

        def step(i, cur, m0, m1, al, last=False):
            nxt = 1 - cur
            finish(nxt, i - 1, al)
            m0, al0 = softmax_step(cur, 0, m0)
            m1, al1 = softmax_step(cur, 1, m1)
            if not last:
                scores_into(par, nxt, i + 1, diag)
            return m0, m1, (al0, al1)

        m0, _ = softmax_step(0, 0, None)
        m1, _ = softmax_step(0, 1, None)
        acc_scr[par] = jnp.zeros_like(acc_scr[par])
        al = (jnp.ones((1, tq), F32),) * 2
        yield
        for i in range(1, nk):
            m0, m1, al = step(i, i % 2, m0, m1, al, last=(i == nk - 1))
            yield
        finish(1, nk - 1, al)
        yield
        a0 = acc_scr[par, 0]
        a1 = acc_scr[par, 1]
        o = a0[0:d2, :] / a0[d2:d2 + 1, :] - lam * (a1[0:d2, :] / a1[d2:d2 + 1, :])
        o = o * lax.rsqrt(jnp.mean(o * o, axis=0, keepdims=True) + NORM_EPS) * g_ref[...] * (1.0 - lam_init)
        o_ref[pl.ds(q0, tq), :] = o.T.astype(BF16)

    def q_loop(j, carry):
        for _ in itertools.zip_longest(q_tile(2 * j, 0), q_tile(2 * j + 1, 1)):
            pass
        return carry

    lax.fori_loop(0, nq // 2, q_loop, 0)


def _attention(qkv, batch, seq, slopes, lq, g_col, lam_init):
    t = qkv.shape[0]
    tk = ATT_TK
    col = lambda off: pl.BlockSpec((seq, 2 * HEAD_DIM), lambda b, h: (b, off + h))
    if seq // tk >= ATT_PREFETCH_MIN_STEPS:
        tq = ATT_TQ
        body = _attn_kernel
        scratch = [pltpu.VMEM((seq, KA_WIDTH), BF16),
                   pltpu.VMEM((seq // tk, VT_ROWS, tk), BF16),
                   pltpu.VMEM((4, KA_WIDTH, tq), BF16),
                   pltpu.VMEM((2, 2, tk, tq), F32),
                   pltpu.VMEM((2, 2, tk, tq), BF16),
                   pltpu.VMEM((2, VT_ROWS, tq), F32)]
    else:
        tq = ATT_TQ_PLAIN
        body = _attn_lockstep_kernel
        scratch = [pltpu.VMEM((2, seq, KA_WIDTH), BF16),
                   pltpu.VMEM((seq // tk, VT_ROWS, tk), BF16),
                   pltpu.VMEM((2, 2, KA_WIDTH, tq), BF16),
                   pltpu.VMEM((2, 2, 2, tk, tq), F32),
                   pltpu.VMEM((2, 2, 2, tk, tq), BF16),
                   pltpu.VMEM((2, 2, VT_ROWS, tq), F32)]
    kernel = functools.partial(body, seq=seq, tq=tq, tk=tk, lam_init=lam_init)
    return pl.pallas_call(
        kernel,
        grid=(batch, ATT_HEADS),
        in_specs=[pl.BlockSpec(memory_space=pltpu.SMEM), col(0), col(ATT_HEADS), col(2 * ATT_HEADS),
                  _full(lq.shape), _full(g_col.shape)],
        out_specs=col(0),
        out_shape=jax.ShapeDtypeStruct((t, ATT_WIDTH), BF16),
        scratch_shapes=scratch,
        compiler_params=_cparams(2),
        name="diff_attention",
    )(slopes, qkv, qkv, qkv, lq, g_col)


def _snap(v):
    r = round(v)
    return float(r) if abs(v - r) < 1e-12 else v


def _fourier_kernel(x_ref, fc_ref, fs_ref, twc_ref, tws_ref, cc_ref, sc_ref, o_ref, *, n2, scale):
    x = x_ref[...]
    tm = fc_ref.shape[0]
    a = _dot(jnp.concatenate([fc_ref[...], fs_ref[...]], axis=0), x)
    ar = a[0:tm, :]
    ai = a[tm:2 * tm, :]
    c = twc_ref[...]
    s = tws_ref[...]
    br = ar * c - ai * s
    bi = -(ar * s + ai * c)
    ccsc = jnp.concatenate([cc_ref[...], sc_ref[...]], axis=0)
    for k2 in range(n2):
        xr = None
        xi = None
        for m in range(n2):
            ph = (m * k2) % n2
            cp = _snap(math.cos(2.0 * math.pi * ph / n2))
            sp = _snap(math.sin(2.0 * math.pi * ph / n2))
            brn = br[:, m * F_WIDTH:(m + 1) * F_WIDTH]
            bin_ = bi[:, m * F_WIDTH:(m + 1) * F_WIDTH]
            tr = brn * cp + bin_ * sp
            ti = bin_ * cp - brn * sp
            xr = tr if xr is None else xr + tr
            xi = ti if xi is None else xi + ti
        out = _dot(jnp.concatenate([xr.astype(BF16), xi.astype(BF16)], axis=1), ccsc) * scale
        o_ref[k2] = out.astype(BF16)


def _fourier(fx, batch, seq, fc, fs, cc, sc):
    n1 = DFT_N1
    n2 = seq // n1
    tm = DFT_TM
    x2 = fx.reshape(batch, n1, n2 * F_WIDTH)
    k1 = lax.broadcasted_iota(jnp.int32, (n1, n2 * F_WIDTH), 0)
    m = lax.broadcasted_iota(jnp.int32, (n1, n2 * F_WIDTH), 1) // F_WIDTH
    ang = ((k1 * m) % seq).astype(F32) * (2.0 * math.pi / seq)
    twc = jnp.cos(ang)
    tws = jnp.sin(ang)
    kernel = functools.partial(_fourier_kernel, n2=n2, scale=1.0 / math.sqrt(seq * F_GW))
    out = pl.pallas_call(
        kernel,
        grid=(n1 // tm, batch),
        in_specs=[pl.BlockSpec((None, n1, n2 * F_WIDTH), lambda i, b: (b, 0, 0)),
                  pl.BlockSpec((tm, n1), lambda i, b: (i, 0)),
                  pl.BlockSpec((tm, n1), lambda i, b: (i, 0)),
                  pl.BlockSpec((tm, n2 * F_WIDTH), lambda i, b: (i, 0)),
                  pl.BlockSpec((tm, n2 * F_WIDTH), lambda i, b: (i, 0)),
                  _full(cc.shape), _full(sc.shape)],
        out_specs=pl.BlockSpec((None, n2, tm, F_WIDTH), lambda i, b: (b, 0, i, 0)),
        out_shape=jax.ShapeDtypeStruct((batch, n2, n1, F_WIDTH), BF16),
        compiler_params=_cparams(2),
        name="fourier",
    )(x2, fc, fs, twc, tws, cc, sc)
    return out.reshape(batch * seq, F_WIDTH)


def _block_diag(w):
    eye = jnp.eye(RG_HEADS, dtype=w.dtype)
    return (eye[:, None, :, None] * w[:, :, None, :]).reshape(RG_WIDTH, RG_WIDTH)


def _dft_tables():
    n = lax.broadcasted_iota(jnp.int32, (DFT_N1, DFT_N1), 0)
    k = lax.broadcasted_iota(jnp.int32, (DFT_N1, DFT_N1), 1)
    ang = ((n * k) % DFT_N1).astype(F32) * (2.0 * math.pi / DFT_N1)
    fc = jnp.cos(ang).astype(BF16)
    fs = jnp.sin(ang).astype(BF16)
    j = lax.broadcasted_iota(jnp.int32, (F_WIDTH, F_WIDTH), 0)
    jp = lax.broadcasted_iota(jnp.int32, (F_WIDTH, F_WIDTH), 1)
    same = (j // F_GW) == (jp // F_GW)
    ang_c = (((j % F_GW) * (jp % F_GW)) % F_GW).astype(F32) * (2.0 * math.pi / F_GW)
    cc = jnp.where(same, jnp.cos(ang_c), 0.0).astype(BF16)
    sc = jnp.where(same, jnp.sin(ang_c), 0.0).astype(BF16)
    return fc, fs, cc, sc


def _trunk(x, batch, seq, p, tables):
    fc, fs, cc, sc = tables
    slopes = 2.0 ** (-8.0 * jnp.arange(1, ATT_HEADS + 1, dtype=F32) / ATT_HEADS)
    for l in range(DEPTH):
        lam_init = 0.8 - 0.6 * math.exp(-0.3 * l)
        x1, rxg, qkv, fx = _ffn_inproj(x, p["ffn1_wg"][l], p["ffn1_wu"][l], p["ffn1_wd"][l],
                                       p["ln_g"][l, 0:1], p["ln_b"][l, 0:1], p["w_in"][l])
        hf, hb = _rglru(rxg, batch, seq, p["conv_w"][l], p["conv_b"][l], p["wgate"][l], p["bgate"][l],
                        p["rg_lambda"][l])
        yb = _attention(qkv, batch, seq, slopes, p["lambda_qk"][l], p["subln_g"][l], lam_init)
        yc = _fourier(fx, batch, seq, fc, fs, cc, sc)
        x = _outproj_ffn(x1, hf, hb, rxg, yb, yc, p["w_out"][l], p["ln_g"][l, 1:2], p["ln_b"][l, 1:2],
                         p["ffn2_wg"][l], p["ffn2_wu"][l], p["ffn2_wd"][l], p["ln_g"][l, 2:3], p["ln_b"][l, 2:3])
    return x


def kernel(x_prompt, x_sample, ln_g, ln_b, ffn1_wg, ffn1_wu, ffn1_wd, ffn2_wg, ffn2_wu, ffn2_wd, w_in, conv_w, conv_b,
           rg_wa, rg_ba, rg_wx, rg_bx, rg_lambda, lambda_qk, subln_g, w_out):
    wgate = jnp.stack([
        jnp.stack([jnp.concatenate([_block_diag(rg_wa[l, d]), _block_diag(rg_wx[l, d])], axis=1)
                   for d in range(2)]) for l in range(DEPTH)]).astype(BF16)
    bgate = jnp.concatenate([rg_ba, rg_bx], axis=-1)[:, :, None, :]
    p = dict(
        ln_g=ln_g, ln_b=ln_b,
        ffn1_wg=ffn1_wg.astype(BF16), ffn1_wu=ffn1_wu.astype(BF16), ffn1_wd=ffn1_wd.astype(BF16),
        ffn2_wg=ffn2_wg.astype(BF16), ffn2_wu=ffn2_wu.astype(BF16), ffn2_wd=ffn2_wd.astype(BF16),
        w_in=w_in.astype(BF16), w_out=w_out.astype(BF16),
        conv_w=conv_w, conv_b=conv_b[:, None, :],
        wgate=wgate, bgate=bgate, rg_lambda=rg_lambda[:, :, None, :],
        lambda_qk=lambda_qk, subln_g=subln_g[:, :, None],
    )
    tables = _dft_tables()
    outs = []
    for x in (x_prompt, x_sample):
        batch, seq, _ = x.shape
        y = _trunk(x.reshape(batch * seq, D_MODEL), batch, seq, p, tables)
        outs.append(y.reshape(batch, seq, D_MODEL))
    return tuple(outs)
```

```python
import functools
import itertools
import math

import numpy as np
import jax
import jax.numpy as jnp
from jax import lax
from jax.experimental import pallas as pl
from jax.experimental.pallas import tpu as pltpu

D_MODEL = 1024
DEPTH = 2
D_FF = 2816
RG_WIDTH = 256
RG_HEADS = 4
RG_BW = 64
RG_C = 8.0
CONV_W = 4
ATT_HEADS = 4
HEAD_DIM = 64
ATT_WIDTH = 512
F_WIDTH = 256
F_GROUPS = 4
F_GW = 64
IN_WIDTH = 2304
ALPHA = (2.0 * DEPTH) ** 0.25
LN_EPS = 1e-5
NORM_EPS = 1e-5

BF16 = jnp.bfloat16
F32 = jnp.float32

VMEM_LIMIT_BYTES = 56 * 1024 * 1024
SUBLANES = 8

TOKEN_TILE = 1024
SUB_TILE = 512
TOKEN_VMEM_LIMIT_BYTES = 62 * 1024 * 1024
FF_CHUNKS = ((0, 1024), (1024, 2048), (2048, 2816))
SCAN_CHUNK = 2048
ATT_TQ = 256
ATT_TQ_PLAIN = 512
ATT_TK = 512
ATT_PREFETCH_MIN_STEPS = 16
POS_BLOCK = 256
KA_WIDTH = 256
Q_PRESCALE = HEAD_DIM ** -0.5 * math.log2(math.e)
VT_ROWS = 144
DFT_N1 = 1024
DFT_TM = 256


def _cparams(n_axes, vmem_limit_bytes=VMEM_LIMIT_BYTES):
    return pltpu.CompilerParams(
        dimension_semantics=("arbitrary",) * n_axes,
        vmem_limit_bytes=vmem_limit_bytes,
    )


def _dot(a, b):
    return jnp.dot(a, b, preferred_element_type=F32)


def _layer_norm(y, g, b):
    mu = jnp.mean(y, axis=-1, keepdims=True)
    yc = y - mu
    var = jnp.mean(yc * yc, axis=-1, keepdims=True)
    return yc * lax.rsqrt(var + LN_EPS) * g + b


def _swiglu(xb, wg_ref, wu_ref, wd_ref):
    hs = []
    for lo, hi in FF_CHUNKS:
        hg = _dot(xb, wg_ref[:, lo:hi])
        hu = _dot(xb, wu_ref[:, lo:hi])
        hs.append((hg * jax.nn.sigmoid(hg) * hu).astype(BF16))
    return _dot(jnp.concatenate(hs, axis=1), wd_ref[...])


def _ffn_inproj_kernel(x_ref, wg_ref, wu_ref, wd_ref, g_ref, b_ref, win_ref,
                       xo_ref, rxg_ref, qkv_ref, fx_ref):
    q_lo, k_lo, v_hi = 2 * RG_WIDTH, 2 * RG_WIDTH + ATT_WIDTH, 2 * RG_WIDTH + 3 * ATT_WIDTH
    subs = [slice(r0, r0 + SUB_TILE) for r0 in range(0, x_ref.shape[0], SUB_TILE)]
    ys = []
    for rows in subs:
        x = x_ref[rows, :]
        ys.append(ALPHA * x + 0.5 * _swiglu(x.astype(BF16), wg_ref, wu_ref, wd_ref))
    for rows, y in zip(subs, ys):
        x1 = _layer_norm(y, g_ref[...], b_ref[...])
        xo_ref[rows, :] = x1
        xb = x1.astype(BF16)
        rxg_ref[rows, :] = _dot(xb, win_ref[:, 0:q_lo])
        qkv_ref[rows, 0:ATT_WIDTH] = (_dot(xb, win_ref[:, q_lo:k_lo]) * Q_PRESCALE).astype(BF16)
        qkv_ref[rows, ATT_WIDTH:3 * ATT_WIDTH] = _dot(xb, win_ref[:, k_lo:v_hi]).astype(BF16)
        fx_ref[rows, :] = _dot(xb, win_ref[:, v_hi:IN_WIDTH]).astype(BF16)


def _full(shape):
    return pl.BlockSpec(shape, lambda *_: (0,) * len(shape))


def _ffn_inproj(x, wg, wu, wd, g, b, win):
    t = x.shape[0]
    tm = TOKEN_TILE
    row = lambda w: pl.BlockSpec((tm, w), lambda i: (i, 0))
    return pl.pallas_call(
        _ffn_inproj_kernel,
        grid=(t // tm,),
        in_specs=[row(D_MODEL), _full(wg.shape), _full(wu.shape), _full(wd.shape),
                  _full(g.shape), _full(b.shape), _full(win.shape)],
        out_specs=[row(D_MODEL), row(2 * RG_WIDTH), row(3 * ATT_WIDTH), row(F_WIDTH)],
        out_shape=[jax.ShapeDtypeStruct((t, D_MODEL), F32),
                   jax.ShapeDtypeStruct((t, 2 * RG_WIDTH), F32),
                   jax.ShapeDtypeStruct((t, 3 * ATT_WIDTH), BF16),
                   jax.ShapeDtypeStruct((t, F_WIDTH), BF16)],
        compiler_params=_cparams(1, TOKEN_VMEM_LIMIT_BYTES),
        name="ffn_inproj",
    )(x, wg, wu, wd, g, b, win)


def _outproj_ffn_kernel(x_ref, hf_ref, hb_ref, rg_ref, yb_ref, yc_ref, wo_ref, g1_ref, b1_ref,
                        wg_ref, wu_ref, wd_ref, g2_ref, b2_ref, o_ref):
    subs = [slice(r0, r0 + SUB_TILE) for r0 in range(0, x_ref.shape[0], SUB_TILE)]
    x2s = []
    for rows in subs:
        ya = (jax.nn.gelu(rg_ref[rows, :], approximate=True) * (hf_ref[rows, :] + hb_ref[rows, :])).astype(BF16)
        mix = _dot(jnp.concatenate([ya, yb_ref[rows, :], yc_ref[rows, :]], axis=1), wo_ref[...])
        x2s.append(_layer_norm(ALPHA * x_ref[rows, :] + mix, g1_ref[...], b1_ref[...]))
    ys = [ALPHA * x2 + 0.5 * _swiglu(x2.astype(BF16), wg_ref, wu_ref, wd_ref) for x2 in x2s]
    for rows, y in zip(subs, ys):
        o_ref[rows, :] = _layer_norm(y, g2_ref[...], b2_ref[...])


def _outproj_ffn(x, hf, hb, rxg, yb, yc, wo, g1, b1, wg, wu, wd, g2, b2):
    t = x.shape[0]
    tm = TOKEN_TILE
    row = lambda w: pl.BlockSpec((tm, w), lambda i: (i, 0))
    rgate = pl.BlockSpec((tm, RG_WIDTH), lambda i: (i, 1))
    return pl.pallas_call(
        _outproj_ffn_kernel,
        grid=(t // tm,),
        in_specs=[row(D_MODEL), row(RG_WIDTH), row(RG_WIDTH), rgate, row(ATT_WIDTH), row(F_WIDTH),
                  _full(wo.shape), _full(g1.shape), _full(b1.shape),
                  _full(wg.shape), _full(wu.shape), _full(wd.shape), _full(g2.shape), _full(b2.shape)],
        out_specs=row(D_MODEL),
        out_shape=jax.ShapeDtypeStruct((t, D_MODEL), F32),
        compiler_params=_cparams(1, TOKEN_VMEM_LIMIT_BYTES),
        name="outproj_ffn",
    )(x, hf, hb, rxg, yb, yc, wo, g1, b1, wg, wu, wd, g2, b2)


def _rglru_direction(x_ref, prev_ref, next_ref, chunk, n_chunks, d, reverse,
                     cw_ref, cb_ref, wgate_ref, bgate_ref, lam_ref,
                     a_scr, u_scr, carry_scr, h_ref):
    tc = x_ref.shape[0]
    n_groups = tc // SUBLANES
    grp = (n_groups, SUBLANES, RG_WIDTH)
    rig = lax.broadcasted_iota(jnp.int32, (1, SUBLANES, 1), 1)
    x = x_ref[...].reshape(grp)
    prev = (prev_ref[...] * jnp.where(chunk > 0, 1.0, 0.0)).reshape(1, SUBLANES, RG_WIDTH)
    nxt = (next_ref[...] * jnp.where(chunk < n_chunks - 1, 1.0, 0.0)).reshape(1, SUBLANES, RG_WIDTH)

    def delayed(k):
        y = pltpu.roll(x, k, 1)
        before = jnp.concatenate([pltpu.roll(prev, k, 1), y[:-1]], axis=0)
        return jnp.where(rig >= k, y, before)

    ahead = pltpu.roll(x, SUBLANES - 1, 1)
    after = jnp.concatenate([ahead[1:], pltpu.roll(nxt, SUBLANES - 1, 1)], axis=0)
    xp1 = jnp.where(rig < SUBLANES - 1, ahead, after)
    cw = cw_ref[...]
    xc = cb_ref[...] + delayed(2) * cw[0:1, :] + delayed(1) * cw[1:2, :] + x * cw[2:3, :] + xp1 * cw[3:4, :]
    xc = xc.reshape(tc, RG_WIDTH)

    gates = _dot(xc.astype(BF16), wgate_ref[d]) + bgate_ref[d]
    r = jax.nn.sigmoid(gates[:, 0:RG_WIDTH])
    i = jax.nn.sigmoid(gates[:, RG_WIDTH:2 * RG_WIDTH])
    nlam = -lam_ref[d]
    softplus = jnp.maximum(nlam, 0.0) + jnp.log1p(jnp.exp(-jnp.abs(nlam)))
    log_a = (-RG_C * softplus) * r
    a = jnp.exp(log_a)
    th = jnp.tanh(log_a)
    u = jnp.sqrt(-2.0 * th / (1.0 - th)) * (i * xc)

    u = u.reshape(grp)
    a = a.reshape(grp)
    for k in (1, 2, 4):
        if reverse:
            keep = rig < SUBLANES - k
            shift = SUBLANES - k
        else:
            keep = rig >= k
            shift = k
        us = jnp.where(keep, pltpu.roll(u, shift, 1), 0.0)
        as_ = jnp.where(keep, pltpu.roll(a, shift, 1), 1.0)
        u = u + a * us
        a = a * as_
    a_scr[...] = a.reshape(tc, RG_WIDTH)
    u_scr[...] = u.reshape(tc, RG_WIDTH)


    def body(g, h):
        gi = n_groups - 1 - g if reverse else g
        r0 = pl.multiple_of(gi * SUBLANES, SUBLANES)
        hg = u_scr[pl.ds(r0, SUBLANES), :] + a_scr[pl.ds(r0, SUBLANES), :] * h
        h_ref[pl.ds(r0, SUBLANES), :] = hg
        edge = hg[0:1, :] if reverse else hg[SUBLANES - 1:SUBLANES, :]
        return jnp.broadcast_to(edge, (SUBLANES, RG_WIDTH))

    carry_scr[...] = lax.fori_loop(0, n_groups, body, carry_scr[...], unroll=8)


def _rglru_kernel(xf_ref, pf_ref, nf_ref, xb_ref, pb_ref, nb_ref,
                  cw_ref, cb_ref, wgate_ref, bgate_ref, lam_ref,
                  hf_ref, hb_ref, af_scr, uf_scr, ab_scr, ub_scr, cf_scr, cbk_scr, *, n_chunks):
    c = pl.program_id(1)

    @pl.when(c == 0)
    def _():
        cf_scr[...] = jnp.zeros_like(cf_scr)
        cbk_scr[...] = jnp.zeros_like(cbk_scr)

    _rglru_direction(xf_ref, pf_ref, nf_ref, c, n_chunks, 0, False,
                     cw_ref, cb_ref, wgate_ref, bgate_ref, lam_ref, af_scr, uf_scr, cf_scr, hf_ref)
    _rglru_direction(xb_ref, pb_ref, nb_ref, n_chunks - 1 - c, n_chunks, 1, True,
                     cw_ref, cb_ref, wgate_ref, bgate_ref, lam_ref, ab_scr, ub_scr, cbk_scr, hb_ref)


def _rglru(rxg, batch, seq, cw, cb, wgate, bgate, lam):
    t = rxg.shape[0]
    tc = SCAN_CHUNK
    nch = seq // tc
    hb8 = tc // SUBLANES
    last8 = t // SUBLANES - 1

    def main_f(b, c):
        return (b * nch + c, 0)

    def prev_f(b, c):
        return (jnp.maximum((b * nch + c) * hb8 - 1, 0), 0)

    def next_f(b, c):
        return (jnp.minimum((b * nch + c + 1) * hb8, last8), 0)

    def main_b(b, c):
        return (b * nch + (nch - 1 - c), 0)

    def prev_b(b, c):
        return (jnp.maximum((b * nch + (nch - 1 - c)) * hb8 - 1, 0), 0)

    def next_b(b, c):
        return (jnp.minimum((b * nch + (nch - 1 - c) + 1) * hb8, last8), 0)

    blk = lambda f: pl.BlockSpec((tc, RG_WIDTH), f)
    halo = lambda f: pl.BlockSpec((SUBLANES, RG_WIDTH), f)
    return pl.pallas_call(
        functools.partial(_rglru_kernel, n_chunks=nch),
        grid=(batch, nch),
        in_specs=[blk(main_f), halo(prev_f), halo(next_f), blk(main_b), halo(prev_b), halo(next_b),
                  _full(cw.shape), _full(cb.shape), _full(wgate.shape), _full(bgate.shape), _full(lam.shape)],
        out_specs=[blk(main_f), blk(main_b)],
        out_shape=[jax.ShapeDtypeStruct((t, RG_WIDTH), F32), jax.ShapeDtypeStruct((t, RG_WIDTH), F32)],
        scratch_shapes=[pltpu.VMEM((tc, RG_WIDTH), F32)] * 4 + [pltpu.VMEM((SUBLANES, RG_WIDTH), F32)] * 2,
        compiler_params=_cparams(2),
        name="rglru",
    )(rxg, rxg, rxg, rxg, rxg, rxg, cw, cb, wgate, bgate, lam)


def _bf16_split3(x):
    out = []
    for _ in range(3):
        hi = float(np.asarray(x, np.float32).astype(jnp.bfloat16).astype(np.float32))
        out.append(hi)
        x = x - hi
    return out


LOG2E_PARTS = _bf16_split3(math.log2(math.e))
EXT_COLS = 12


def _attn_kernel(slopes_ref, q_ref, k_ref, v_ref, lq_ref, g_ref, o_ref, ka_scr, vt_scr, qa_scr, s_scr, p_scr, acc_scr,
                 *, seq, tq, tk, lam_init):
    h = pl.program_id(1)
    nk = seq // tk
    nq = seq // tq
    prefetch_head = nk >= ATT_PREFETCH_MIN_STEPS
    slope = slopes_ref[h]
    d2 = 2 * HEAD_DIM
    l1, l2, l3 = LOG2E_PARTS

    def log2e_piece(idx):
        r = idx % 3
        return jnp.where(r == 0, l1, jnp.where(r == 1, l2, l3))

    ka_scr[:, 0:d2] = k_ref[...]
    pos = lax.broadcasted_iota(jnp.int32, (seq, d2), 0)
    col = lax.broadcasted_iota(jnp.int32, (seq, d2), 1)
    rem = pos & (POS_BLOCK - 1)
    ext = jnp.where(col < 6, log2e_piece(col),
                    jnp.where(col < 9, rem.astype(F32) * slope,
                              jnp.where(col < EXT_COLS, (pos - rem).astype(F32) * slope, 0.0)))
    ka_scr[:, d2:2 * d2] = ext.astype(BF16)
    ones_rows = jnp.where(lax.broadcasted_iota(jnp.int32, (VT_ROWS - d2, tk), 0) == 0, 1.0, 0.0).astype(BF16)
    for j in range(nk):
        vt_scr[j, 0:d2, :] = v_ref[j * tk:(j + 1) * tk, :].astype(F32).T.astype(BF16)
        vt_scr[j, d2:VT_ROWS, :] = ones_rows

    lq = lq_ref[...]
    lam = (jnp.exp(jnp.sum(lq[0:1, :] * lq[1:2, :], axis=1, keepdims=True))
           - jnp.exp(jnp.sum(lq[2:3, :] * lq[3:4, :], axis=1, keepdims=True)) + lam_init)
    rowi = lax.broadcasted_iota(jnp.int32, (d2, tq), 0)
    coli = lax.broadcasted_iota(jnp.int32, (d2, tq), 1)

    def tile_of(i, diag):
        t = i - 1
        return jnp.where(i == 0, diag, t + jnp.where(t >= diag, 1, 0))

    def keys(tile):
        return ka_scr[pl.ds(pl.multiple_of(tile * tk, tk), tk), :]

    def scores_into(slot, i, diag):
        tile = tile_of(i, diag)
        ka = keys(tile)
        var = jnp.where(tile > diag, 1, 0)
        for c in range(2):
            s_scr[slot, c] = _dot(ka, qa_scr[2 * var + c])

    def head(qi):
        q0 = pl.multiple_of(qi * tq, tq)
        qt = q_ref[pl.ds(q0, tq), :].astype(F32).T
        posq = q0 + coli
        remq = posq & (POS_BLOCK - 1)
        ext_t = jnp.where(rowi < 3, -(remq.astype(F32) * slope),
                          jnp.where(rowi < 6, -((posq - remq).astype(F32) * slope),
                                    jnp.where(rowi < EXT_COLS, log2e_piece(rowi), 0.0)))
        for c in range(2):
            qc_t = jnp.where((rowi >= c * HEAD_DIM) & (rowi < (c + 1) * HEAD_DIM), qt, 0.0).astype(BF16)
            for var in range(2):
                qa_scr[2 * var + c, 0:d2, :] = qc_t
                qa_scr[2 * var + c, d2:2 * d2, :] = (ext_t if var == 0 else -ext_t).astype(BF16)
        diag = q0 // tk
        kd = keys(diag)
        for c in range(2):
            s_scr[0, c] = jnp.minimum(_dot(kd, qa_scr[c]), _dot(kd, qa_scr[2 + c]))
        scores_into(1, 1, diag)

    def q_tile(qi):
        q0 = pl.multiple_of(qi * tq, tq)
        diag = q0 // tk
        if not prefetch_head:
            head(qi)

        def softmax_step(slot, c, m):
            s = s_scr[slot, c]
            mn = jnp.max(s, axis=0, keepdims=True)
            al = None
            if m is not None:
                mn = jnp.maximum(m, mn)
                al = jnp.exp2(m - mn)
            p_scr[slot, c] = jnp.exp2(s - mn).astype(BF16)
            return mn, al

        def finish(slot, i, al):
            vt = vt_scr[tile_of(i, diag)]
            for c in range(2):
                acc_scr[c] = al[c] * acc_scr[c] + _dot(vt, p_scr[slot, c])

        def step(i, cur, m0, m1, al, last=False):
            nxt = 1 - cur
            finish(nxt, i - 1, al)
            m0, al0 = softmax_step(cur, 0, m0)
            m1, al1 = softmax_step(cur, 1, m1)
            if not last:
                scores_into(nxt, i + 1, diag)
            return m0, m1, (al0, al1)

        m0, _ = softmax_step(0, 0, None)
        m1, _ = softmax_step(0, 1, None)
        acc_scr[...] = jnp.zeros_like(acc_scr)
        al = (jnp.ones((1, tq), F32),) * 2
        for i in range(1, nk):
            m0, m1, al = step(i, i % 2, m0, m1, al, last=(i == nk - 1))
        finish(1, nk - 1, al)
        if prefetch_head:
            head(jnp.minimum(qi + 1, nq - 1))
        a0 = acc_scr[0]
        a1 = acc_scr[1]
        o = a0[0:d2, :] / a0[d2:d2 + 1, :] - lam * (a1[0:d2, :] / a1[d2:d2 + 1, :])
        o = o * lax.rsqrt(jnp.mean(o * o, axis=0, keepdims=True) + NORM_EPS) * g_ref[...] * (1.0 - lam_init)
        o_ref[pl.ds(q0, tq), :] = o.T.astype(BF16)

    if prefetch_head:
        head(0)

    def q_loop(qi, carry):
        q_tile(qi)
        return carry

    lax.fori_loop(0, nq, q_loop, 0)


def _dot_nt(a, b):
    return lax.dot_general(a, b, (((1,), (1,)), ((), ())), preferred_element_type=F32)


def _attn_lockstep_kernel(slopes_ref, q_ref, k_ref, v_ref, lq_ref, g_ref, o_ref,
                          ka_scr, vt_scr, qa_scr, s_scr, p_scr, acc_scr, *, seq, tq, tk, lam_init):
    h = pl.program_id(1)
    nk = seq // tk
    nq = seq // tq
    slope = slopes_ref[h]
    d2 = 2 * HEAD_DIM
    l1, l2, l3 = LOG2E_PARTS

    def log2e_piece(idx):
        r = idx % 3
        return jnp.where(r == 0, l1, jnp.where(r == 1, l2, l3))

    pos = lax.broadcasted_iota(jnp.int32, (seq, d2), 0)
    col = lax.broadcasted_iota(jnp.int32, (seq, d2), 1)
    rem = pos & (POS_BLOCK - 1)
    ext = jnp.where(col < 6, log2e_piece(col),
                    jnp.where(col < 9, rem.astype(F32) * slope,
                              jnp.where(col < EXT_COLS, (pos - rem).astype(F32) * slope, 0.0)))
    for var in range(2):
        ka_scr[var, :, 0:d2] = k_ref[...]
        ka_scr[var, :, d2:2 * d2] = (ext if var == 0 else -ext).astype(BF16)
    ones_rows = jnp.where(lax.broadcasted_iota(jnp.int32, (VT_ROWS - d2, tk), 0) == 0, 1.0, 0.0).astype(BF16)
    for j in range(nk):
        vt_scr[j, 0:d2, :] = v_ref[j * tk:(j + 1) * tk, :].astype(F32).T.astype(BF16)
        vt_scr[j, d2:VT_ROWS, :] = ones_rows

    lq = lq_ref[...]
    lam = (jnp.exp(jnp.sum(lq[0:1, :] * lq[1:2, :], axis=1, keepdims=True))
           - jnp.exp(jnp.sum(lq[2:3, :] * lq[3:4, :], axis=1, keepdims=True)) + lam_init)
    rowi = lax.broadcasted_iota(jnp.int32, (d2, tq), 0)
    coli = lax.broadcasted_iota(jnp.int32, (d2, tq), 1)
    eye = (lax.broadcasted_iota(jnp.int32, (d2, d2), 0) == lax.broadcasted_iota(jnp.int32, (d2, d2), 1)).astype(BF16)

    def tile_of(i, diag):
        t = i - 1
        return jnp.where(i == 0, diag, t + jnp.where(t >= diag, 1, 0))

    def keys(var, tile):
        return ka_scr[var, pl.ds(pl.multiple_of(tile * tk, tk), tk), :]

    def scores_into(par, slot, i, diag):
        tile = tile_of(i, diag)
        ka = keys(jnp.where(tile > diag, 1, 0), tile)
        for c in range(2):
            s_scr[par, slot, c] = _dot(ka, qa_scr[par, c])

    def head(qi, par):
        q0 = pl.multiple_of(qi * tq, tq)
        qt = _dot_nt(eye, q_ref[pl.ds(q0, tq), :])
        posq = q0 + coli
        remq = posq & (POS_BLOCK - 1)
        ext_t = jnp.where(rowi < 3, -(remq.astype(F32) * slope),
                          jnp.where(rowi < 6, -((posq - remq).astype(F32) * slope),
                                    jnp.where(rowi < EXT_COLS, log2e_piece(rowi), 0.0))).astype(BF16)
        for c in range(2):
            qa_scr[par, c, 0:d2, :] = jnp.where((rowi >= c * HEAD_DIM) & (rowi < (c + 1) * HEAD_DIM),
                                                qt, 0.0).astype(BF16)
            qa_scr[par, c, d2:2 * d2, :] = ext_t
        diag = q0 // tk
        for c in range(2):
            s_scr[par, 0, c] = jnp.minimum(_dot(keys(0, diag), qa_scr[par, c]), _dot(keys(1, diag), qa_scr[par, c]))
        scores_into(par, 1, 1, diag)

    def q_tile(qi, par):
        q0 = pl.multiple_of(qi * tq, tq)
        diag = q0 // tk
        head(qi, par)
        yield

        def softmax_step(slot, c, m):
            s = s_scr[par, slot, c]
            mn = jnp.max(s, axis=0, keepdims=True)
            al = None
            if m is not None:
                mn = jnp.maximum(m, mn)
                al = jnp.exp2(m - mn)
            p_scr[par, slot, c] = jnp.exp2(s - mn).astype(BF16)
            return mn, al

        def finish(slot, i, al):
            vt = vt_scr[tile_of(i, diag)]
            for c in range(2):
                acc_scr[par, c] = al[c] * acc_scr[par, c] + _dot(vt, p_scr[par, slot, c])

        def step(i, cur, m0, m1, al, last=False):
            nxt = 1 - cur
            finish(nxt, i - 1, al)
            m0, al0 = softmax_step(cur, 0, m0)
            m1, al1 = softmax_step(cur, 1, m1)
            if not last:
                scores_into(par, nxt, i + 1, diag)
            return m0, m1, (al0, al1)

        m0, _ = softmax_step(0, 0, None)
        m1, _ = softmax_step(0, 1, None)
        acc_scr[par] = jnp.zeros_like(acc_scr[par])
        al = (jnp.ones((1, tq), F32),) * 2
        yield
        for i in range(1, nk):
            m0, m1, al = step(i, i % 2, m0, m1, al, last=(i == nk - 1))
            yield
        finish(1, nk - 1, al)
        yield
        a0 = acc_scr[par, 0]
        a1 = acc_scr[par, 1]
        o = a0[0:d2, :] / a0[d2:d2 + 1, :] - lam * (a1[0:d2, :] / a1[d2:d2 + 1, :])
        o = o * lax.rsqrt(jnp.mean(o * o, axis=0, keepdims=True) + NORM_EPS) * g_ref[...] * (1.0 - lam_init)
        o_ref[pl.ds(q0, tq), :] = o.T.astype(BF16)

    def q_loop(j, carry):
        for _ in itertools.zip_longest(q_tile(2 * j, 0), q_tile(2 * j + 1, 1)):
            pass
        return carry

    lax.fori_loop(0, nq // 2, q_loop, 0)


def _attention(qkv, batch, seq, slopes, lq, g_col, lam_init):
    t = qkv.shape[0]
    tk = ATT_TK
    col = lambda off: pl.BlockSpec((seq, 2 * HEAD_DIM), lambda b, h: (b, off + h))
    if seq // tk >= ATT_PREFETCH_MIN_STEPS:
        tq = ATT_TQ
        body = _attn_kernel
        scratch = [pltpu.VMEM((seq, KA_WIDTH), BF16),
                   pltpu.VMEM((seq // tk, VT_ROWS, tk), BF16),
                   pltpu.VMEM((4, KA_WIDTH, tq), BF16),
                   pltpu.VMEM((2, 2, tk, tq), F32),
                   pltpu.VMEM((2, 2, tk, tq), BF16),
                   pltpu.VMEM((2, VT_ROWS, tq), F32)]
    else:
        tq = ATT_TQ_PLAIN
        body = _attn_lockstep_kernel
        scratch = [pltpu.VMEM((2, seq, KA_WIDTH), BF16),
                   pltpu.VMEM((seq // tk, VT_ROWS, tk), BF16),
                   pltpu.VMEM((2, 2, KA_WIDTH, tq), BF16),
                   pltpu.VMEM((2, 2, 2, tk, tq), F32),
                   pltpu.VMEM((2, 2, 2, tk, tq), BF16),
                   pltpu.VMEM((2, 2, VT_ROWS, tq), F32)]
    kernel = functools.partial(body, seq=seq, tq=tq, tk=tk, lam_init=lam_init)
    return pl.pallas_call(
        kernel,
        grid=(batch, ATT_HEADS),
        in_specs=[pl.BlockSpec(memory_space=pltpu.SMEM), col(0), col(ATT_HEADS), col(2 * ATT_HEADS),
                  _full(lq.shape), _full(g_col.shape)],
        out_specs=col(0),
        out_shape=jax.ShapeDtypeStruct((t, ATT_WIDTH), BF16),
        scratch_shapes=scratch,
        compiler_params=_cparams(2),
        name="diff_attention",
    )(slopes, qkv, qkv, qkv, lq, g_col)


def _snap(v):
    r = round(v)
    return float(r) if abs(v - r) < 1e-12 else v


def _fourier_kernel(x_ref, fc_ref, fs_ref, twc_ref, tws_ref, cc_ref, sc_ref, o_ref, *, n2, scale):
    x = x_ref[...]
    tm = fc_ref.shape[0]
    a = _dot(jnp.concatenate([fc_ref[...], fs_ref[...]], axis=0), x)
    ar = a[0:tm, :]
    ai = a[tm:2 * tm, :]
    c = twc_ref[...]
    s = tws_ref[...]
    br = ar * c - ai * s
    bi = -(ar * s + ai * c)
    ccsc = jnp.concatenate([cc_ref[...], sc_ref[...]], axis=0)
    for k2 in range(n2):
        xr = None
        xi = None
        for m in range(n2):
            ph = (m * k2) % n2
            cp = _snap(math.cos(2.0 * math.pi * ph / n2))
            sp = _snap(math.sin(2.0 * math.pi * ph / n2))
            brn = br[:, m * F_WIDTH:(m + 1) * F_WIDTH]
            bin_ = bi[:, m * F_WIDTH:(m + 1) * F_WIDTH]
            tr = brn * cp + bin_ * sp
            ti = bin_ * cp - brn * sp
            xr = tr if xr is None else xr + tr
            xi = ti if xi is None else xi + ti
        out = _dot(jnp.concatenate([xr.astype(BF16), xi.astype(BF16)], axis=1), ccsc) * scale
        o_ref[k2] = out.astype(BF16)


def _fourier(fx, batch, seq, fc, fs, cc, sc):
    n1 = DFT_N1
    n2 = seq // n1
    tm = DFT_TM
    x2 = fx.reshape(batch, n1, n2 * F_WIDTH)
    k1 = lax.broadcasted_iota(jnp.int32, (n1, n2 * F_WIDTH), 0)
    m = lax.broadcasted_iota(jnp.int32, (n1, n2 * F_WIDTH), 1) // F_WIDTH
    ang = ((k1 * m) % seq).astype(F32) * (2.0 * math.pi / seq)
    twc = jnp.cos(ang)
    tws = jnp.sin(ang)
    kernel = functools.partial(_fourier_kernel, n2=n2, scale=1.0 / math.sqrt(seq * F_GW))
    out = pl.pallas_call(
        kernel,
        grid=(n1 // tm, batch),
        in_specs=[pl.BlockSpec((None, n1, n2 * F_WIDTH), lambda i, b: (b, 0, 0)),
                  pl.BlockSpec((tm, n1), lambda i, b: (i, 0)),
                  pl.BlockSpec((tm, n1), lambda i, b: (i, 0)),
                  pl.BlockSpec((tm, n2 * F_WIDTH), lambda i, b: (i, 0)),
                  pl.BlockSpec((tm, n2 * F_WIDTH), lambda i, b: (i, 0)),
                  _full(cc.shape), _full(sc.shape)],
        out_specs=pl.BlockSpec((None, n2, tm, F_WIDTH), lambda i, b: (b, 0, i, 0)),
        out_shape=jax.ShapeDtypeStruct((batch, n2, n1, F_WIDTH), BF16),
        compiler_params=_cparams(2),
        name="fourier",
    )(x2, fc, fs, twc, tws, cc, sc)
    return out.reshape(batch * seq, F_WIDTH)


def _block_diag(w):
    eye = jnp.eye(RG_HEADS, dtype=w.dtype)
    return (eye[:, None, :, None] * w[:, :, None, :]).reshape(RG_WIDTH, RG_WIDTH)


def _dft_tables():
    n = lax.broadcasted_iota(jnp.int32, (DFT_N1, DFT_N1), 0)
    k = lax.broadcasted_iota(jnp.int32, (DFT_N1, DFT_N1), 1)
    ang = ((n * k) % DFT_N1).astype(F32) * (2.0 * math.pi / DFT_N1)
    fc = jnp.cos(ang).astype(BF16)
    fs = jnp.sin(ang).astype(BF16)
    j = lax.broadcasted_iota(jnp.int32, (F_WIDTH, F_WIDTH), 0)
    jp = lax.broadcasted_iota(jnp.int32, (F_WIDTH, F_WIDTH), 1)
    same = (j // F_GW) == (jp // F_GW)
    ang_c = (((j % F_GW) * (jp % F_GW)) % F_GW).astype(F32) * (2.0 * math.pi / F_GW)
    cc = jnp.where(same, jnp.cos(ang_c), 0.0).astype(BF16)
    sc = jnp.where(same, jnp.sin(ang_c), 0.0).astype(BF16)
    return fc, fs, cc, sc


def _trunk(x, batch, seq, p, tables):
    fc, fs, cc, sc = tables
    slopes = 2.0 ** (-8.0 * jnp.arange(1, ATT_HEADS + 1, dtype=F32) / ATT_HEADS)
    for l in range(DEPTH):
        lam_init = 0.8 - 0.6 * math.exp(-0.3 * l)
        x1, rxg, qkv, fx = _ffn_inproj(x, p["ffn1_wg"][l], p["ffn1_wu"][l], p["ffn1_wd"][l],
                                       p["ln_g"][l, 0:1], p["ln_b"][l, 0:1], p["w_in"][l])
        hf, hb = _rglru(rxg, batch, seq, p["conv_w"][l], p["conv_b"][l], p["wgate"][l], p["bgate"][l],
                        p["rg_lambda"][l])
        yb = _attention(qkv, batch, seq, slopes, p["lambda_qk"][l], p["subln_g"][l], lam_init)
        yc = _fourier(fx, batch, seq, fc, fs, cc, sc)
        x = _outproj_ffn(x1, hf, hb, rxg, yb, yc, p["w_out"][l], p["ln_g"][l, 1:2], p["ln_b"][l, 1:2],
                         p["ffn2_wg"][l], p["ffn2_wu"][l], p["ffn2_wd"][l], p["ln_g"][l, 2:3], p["ln_b"][l, 2:3])
    return x


def kernel(x_prompt, x_sample, ln_g, ln_b, ffn1_wg, ffn1_wu, ffn1_wd, ffn2_wg, ffn2_wu, ffn2_wd, w_in, conv_w, conv_b,
           rg_wa, rg_ba, rg_wx, rg_bx, rg_lambda, lambda_qk, subln_g, w_out):
    wgate = jnp.stack([
        jnp.stack([jnp.concatenate([_block_diag(rg_wa[l, d]), _block_diag(rg_wx[l, d])], axis=1)
                   for d in range(2)]) for l in range(DEPTH)]).astype(BF16)
    bgate = jnp.concatenate([rg_ba, rg_bx], axis=-1)[:, :, None, :]
    p = dict(
        ln_g=ln_g, ln_b=ln_b,
        ffn1_wg=ffn1_wg.astype(BF16), ffn1_wu=ffn1_wu.astype(BF16), ffn1_wd=ffn1_wd.astype(BF16),
        ffn2_wg=ffn2_wg.astype(BF16), ffn2_wu=ffn2_wu.astype(BF16), ffn2_wd=ffn2_wd.astype(BF16),
        w_in=w_in.astype(BF16), w_out=w_out.astype(BF16),
        conv_w=conv_w, conv_b=conv_b[:, None, :],
        wgate=wgate, bgate=bgate, rg_lambda=rg_lambda[:, :, None, :],
        lambda_qk=lambda_qk, subln_g=subln_g[:, :, None],
    )
    tables = _dft_tables()
    outs = []
    for x in (x_prompt, x_sample):
        batch, seq, _ = x.shape
        y = _trunk(x.reshape(batch * seq, D_MODEL), batch, seq, p, tables)
        outs.append(y.reshape(batch, seq, D_MODEL))
    return tuple(outs)
```
